```python
import jax, jax.numpy as jnp
from jax import lax
import numpy as np

D_MODEL = 1024
BATCH = 16
SEQ = 4096
DEPTH = 1

PLE_DIM = 256
HG_WIDTH = D_MODEL // 2
HG_KDIM = 128
HG_HEADS = HG_WIDTH // HG_KDIM
HG_VDIM = HG_WIDTH // HG_HEADS
HG_CHUNK = 64
FOX_WIDTH = D_MODEL // 2
FOX_HDIM = 64
FOX_HEADS = FOX_WIDTH // FOX_HDIM
FOX_BLOCK = 128
D_FF = 4 * D_MODEL
N_IN = 4 * HG_WIDTH + 3 * FOX_WIDTH + FOX_HEADS + 2 * D_MODEL
LN_EPS = 1e-5
RMS_EPS = 1e-6
DEEPNORM_ALPHA = (2.0 * DEPTH) ** 0.25
DEEPNORM_BETA = (8.0 * DEPTH) ** -0.25

kernel_name = "hybrid_hgrn2_fox_deepnorm_block"


def layer_norm(x, g, b):
    xf = x.astype(jnp.float32)
    mu = jnp.mean(xf, axis=-1, keepdims=True)
    var = jnp.mean(jnp.square(xf - mu), axis=-1, keepdims=True)
    y = (xf - mu) * lax.rsqrt(var + LN_EPS)
    return (y * g.astype(jnp.float32) + b.astype(jnp.float32)).astype(x.dtype)


def hgrn2_lower_bound(lb_logits, layer):
    probs = jax.nn.softmax(lb_logits.astype(jnp.float32), axis=0)
    return jnp.cumsum(probs, axis=0)[layer]


def _hgrn2_chunk_step(state, chunk):
    q_c, k_c, v_c, g_c = chunk
    C = q_c.shape[2]
    b = jnp.cumsum(g_c, axis=2)
    causal = jnp.tril(jnp.ones((C, C), dtype=bool))[:, :, None]
    diff = b[:, :, :, None, :] - b[:, :, None, :, :]
    decay = jnp.exp(jnp.where(causal, diff, -jnp.inf))
    scores = jnp.einsum('bhtk,bhtsk,bhsk->bhts', q_c, decay, k_c)
    o = (jnp.einsum('bhts,bhsv->bhtv', scores, v_c)
         + jnp.einsum('bhtk,bhkv->bhtv', q_c * jnp.exp(b), state))
    b_last = b[:, :, -1:, :]
    new_state = (jnp.exp(b_last[:, :, 0, :])[..., None] * state
                 + jnp.einsum('bhsk,bhsv->bhkv', k_c * jnp.exp(b_last - b), v_c))
    return new_state, o


def hgrn2_branch(q_lin, f_lin, i_lin, og_lin, lb, norm_g):
    B, S, _ = q_lin.shape
    n_chunks = S // HG_CHUNK
    q = jax.nn.silu(q_lin.astype(jnp.float32))
    f = lb + (1.0 - lb) * jax.nn.sigmoid(f_lin.astype(jnp.float32))
    log_f = jnp.log(f)
    k = 1.0 - f
    v = i_lin.astype(jnp.float32)

    def to_chunks(t, d):
        return t.reshape(B, n_chunks, HG_CHUNK, HG_HEADS, d).transpose(1, 0, 3, 2, 4)

    xs = (to_chunks(q, HG_KDIM), to_chunks(k, HG_KDIM), to_chunks(v, HG_VDIM), to_chunks(log_f, HG_KDIM))
    state0 = jnp.zeros((B, HG_HEADS, HG_KDIM, HG_VDIM), jnp.float32)
    _, o = lax.scan(_hgrn2_chunk_step, state0, xs)
    o = o.transpose(1, 0, 3, 2, 4).reshape(B, S, HG_HEADS, HG_VDIM)
    o = o * lax.rsqrt(jnp.mean(jnp.square(o), axis=-1, keepdims=True) + RMS_EPS)
    o = o.reshape(B, S, HG_WIDTH) * norm_g.astype(jnp.float32) * jax.nn.silu(og_lin.astype(jnp.float32))
    return o.astype(q_lin.dtype)


def fox_branch(q_lin, k_lin, v_lin, f_lin, f_bias):
    B, S, _ = q_lin.shape
    def heads(t):
        return t.reshape(B, S, FOX_HEADS, FOX_HDIM).transpose(0, 2, 1, 3)
    q, k, v = heads(q_lin), heads(k_lin), heads(v_lin)
    log_f = jax.nn.log_sigmoid(f_lin.astype(jnp.float32) + f_bias.astype(jnp.float32))
    c = jnp.cumsum(log_f.transpose(0, 2, 1), axis=-1)
    scale = FOX_HDIM ** -0.5
    outs = []
    for blk in range(S // FOX_BLOCK):
        start, end = blk * FOX_BLOCK, (blk + 1) * FOX_BLOCK
        qb = q[:, :, start:end]
        kb, vb = k[:, :, :end], v[:, :, :end]
        logits = jnp.einsum('bhqd,bhkd->bhqk', qb, kb).astype(jnp.float32) * scale
        logits = logits + (c[:, :, start:end, None] - c[:, :, None, :end])
        q_pos = jnp.arange(start, end)
        k_pos = jnp.arange(end)
        mask = k_pos[None, :] <= q_pos[:, None]
        probs = jax.nn.softmax(jnp.where(mask, logits, -jnp.inf), axis=-1)
        outs.append(jnp.einsum('bhqk,bhkd->bhqd', probs.astype(vb.dtype), vb))
    o = jnp.concatenate(outs, axis=2)
    return o.transpose(0, 2, 1, 3).reshape(B, S, FOX_WIDTH)


def setup_inputs(seed: int = 0) -> dict:
    key = jax.random.key(seed)
    ks = jax.random.split(key, 20)
    f32 = jnp.float32
    nrm = lambda k, shape, s: jax.random.normal(k, shape, f32) * s
    beta = DEEPNORM_BETA
    return {
        "x": nrm(ks[0], (BATCH, SEQ, D_MODEL), 1.0),
        "p": nrm(ks[1], (DEPTH, BATCH, SEQ, PLE_DIM), 1.0),
        "ln0_g": 1.0 + nrm(ks[2], (D_MODEL,), 0.02),
        "ln0_b": nrm(ks[3], (D_MODEL,), 0.02),
        "w_in": nrm(ks[4], (DEPTH, D_MODEL, N_IN), D_MODEL ** -0.5),
        "hg_lb": nrm(ks[5], (DEPTH + 1, HG_WIDTH), 1.0),
        "hg_norm_g": 1.0 + nrm(ks[6], (DEPTH, HG_WIDTH), 0.02),
        "fox_fb": jnp.linspace(1.0, 4.0, FOX_HEADS, dtype=f32)[None, :] + nrm(ks[7], (DEPTH, FOX_HEADS), 0.1),
        "w_a": nrm(ks[8], (DEPTH, HG_WIDTH, D_MODEL), beta * HG_WIDTH ** -0.5),
        "w_b": nrm(ks[9], (DEPTH, FOX_WIDTH, D_MODEL), beta * FOX_WIDTH ** -0.5),
        "w_o": nrm(ks[10], (DEPTH, D_MODEL, D_MODEL), beta * D_MODEL ** -0.5),
        "ln1_g": 1.0 + nrm(ks[11], (DEPTH, D_MODEL), 0.02),
        "ln1_b": nrm(ks[12], (DEPTH, D_MODEL), 0.02),
        "w_ff1": nrm(ks[13], (DEPTH, D_MODEL, D_FF), beta * D_MODEL ** -0.5),
        "w_ff2": nrm(ks[14], (DEPTH, D_FF, D_MODEL), beta * D_FF ** -0.5),
        "w_pg": nrm(ks[15], (DEPTH, D_MODEL, D_MODEL), D_MODEL ** -0.5),
        "w_p": nrm(ks[16], (DEPTH, PLE_DIM, D_MODEL), beta * PLE_DIM ** -0.5),
        "ln2_g": 1.0 + nrm(ks[17], (DEPTH, D_MODEL), 0.02),
        "ln2_b": nrm(ks[18], (DEPTH, D_MODEL), 0.02),
    }


def reference(x, p, ln0_g, ln0_b, w_in, hg_lb, hg_norm_g, fox_fb, w_a, w_b, w_o,
              ln1_g, ln1_b, w_ff1, w_ff2, w_pg, w_p, ln2_g, ln2_b):
    sizes = [HG_WIDTH] * 4 + [FOX_WIDTH] * 3 + [FOX_HEADS] + [D_MODEL] * 2
    split_at = [int(s) for s in np.cumsum(sizes)[:-1]]
    h = layer_norm(x, ln0_g, ln0_b)
    for i in range(DEPTH):
        proj = jnp.einsum('bsd,dn->bsn', h, w_in[i])
        (hg_q, hg_f, hg_i, hg_g, fx_q, fx_k, fx_v, fx_f, gate_a, gate_b) = jnp.split(proj, split_at, axis=-1)
        lb = hgrn2_lower_bound(hg_lb, i)
        y_a = hgrn2_branch(hg_q, hg_f, hg_i, hg_g, lb, hg_norm_g[i])
        y_b = fox_branch(fx_q, fx_k, fx_v, fx_f, fox_fb[i])
        merged = (jax.nn.sigmoid(gate_a) * jnp.einsum('bsc,cd->bsd', y_a, w_a[i])
                  + jax.nn.sigmoid(gate_b) * jnp.einsum('bsc,cd->bsd', y_b, w_b[i]))
        mix = jnp.einsum('bsd,de->bse', merged, w_o[i])
        h = layer_norm(DEEPNORM_ALPHA * h + mix, ln1_g[i], ln1_b[i])
        ff = jnp.einsum('bsf,fd->bsd', jnp.square(jax.nn.relu(jnp.einsum('bsd,df->bsf', h, w_ff1[i]))), w_ff2[i])
        ple = jax.nn.sigmoid(jnp.einsum('bsd,de->bse', h, w_pg[i])) * jnp.einsum('bsp,pd->bsd', p[i], w_p[i])
        h = layer_norm(DEEPNORM_ALPHA * h + ff + ple, ln2_g[i], ln2_b[i])
    return h
```

```python
import functools

import numpy as np
import jax
import jax.numpy as jnp
from jax import lax
from jax.experimental import pallas as pl
from jax.experimental.pallas import tpu as pltpu

F32 = jnp.float32
BF16 = jnp.bfloat16

D_MODEL = 1024
HG_WIDTH = 512
HG_KDIM = 128
HG_HEADS = HG_WIDTH // HG_KDIM
FOX_WIDTH = 512
FOX_HDIM = 64
FOX_HEADS = FOX_WIDTH // FOX_HDIM
D_FF = 4 * D_MODEL
PLE_DIM = 256
LN_EPS = 1e-5
RMS_EPS = 1e-6
LANES = 128
HG_CHUNK = 64
HG_LEVELS = 6
NEG_BIG = -1e30
VMEM_LIMIT = 56 * 1024 * 1024


def _layer_norm(x, g, b):
    mu = jnp.mean(x, axis=-1, keepdims=True)
    xc = x - mu
    var = jnp.mean(xc * xc, axis=-1, keepdims=True)
    return xc * lax.rsqrt(var + LN_EPS) * g + b


def _sigmoid_pair(x):
    e = jnp.exp(-jnp.abs(x))
    r = 1.0 / (1.0 + e)
    er = e * r
    pos = x >= 0
    return jnp.where(pos, r, er), jnp.where(pos, er, r)


def _split3(x):
    hi = x.astype(BF16)
    r = x - hi.astype(F32)
    mid = r.astype(BF16)
    lo = (r - mid.astype(F32)).astype(BF16)
    return hi, mid, lo


def _dot(a, b):
    return jnp.dot(a, b, preferred_element_type=F32)


def _dot_nt(a, b):
    return lax.dot_general(a, b, (((1,), (1,)), ((), ())), preferred_element_type=F32)


def _dot_tn(a, b):
    return lax.dot_general(a, b, (((0,), (0,)), ((), ())), preferred_element_type=F32)


def _inproj_kernel(x_ref, g0_ref, b0_ref, lb_ref, fb_ref,
                   w_hq, w_hf, w_hi, w_hg, w_fq, w_fkt, w_fv, w_ff, w_ga, w_gb,
                   qa_ref, g_ref, kk_ref, v_ref, og_ref,
                   fq_ref, fkt_ref, fv_ref, lf_ref, sga_ref, sgb_ref):
    h = _layer_norm(x_ref[...], g0_ref[...], b0_ref[...]).astype(BF16)

    lbl = lb_ref[...]
    e = jnp.exp(lbl - jnp.max(lbl, axis=0, keepdims=True))
    lb = e[0:1, :] / jnp.sum(e, axis=0, keepdims=True)

    q = _dot(h, w_hq[...])
    sq, _ = _sigmoid_pair(q)
    qa_ref[...] = (q * sq).astype(BF16)

    sf, snf = _sigmoid_pair(_dot(h, w_hf[...]))
    g_ref[...] = jnp.log(lb + (1.0 - lb) * sf)
    kk_ref[...] = ((1.0 - lb) * snf).astype(BF16)

    v_ref[...] = _dot(h, w_hi[...]).astype(BF16)

    og = _dot(h, w_hg[...])
    so, _ = _sigmoid_pair(og)
    og_ref[...] = (og * so).astype(BF16)

    fq_ref[...] = (_dot(h, w_fq[...]) * (FOX_HDIM ** -0.5)).astype(BF16)
    fkt_ref[...] = _dot_nt(w_fkt[...], h).astype(BF16)
    fv_ref[...] = _dot(h, w_fv[...]).astype(BF16)

    z = _dot(h, w_ff[...]) + fb_ref[...]
    lf_ref[...] = jnp.minimum(z, 0.0) - jnp.log(1.0 + jnp.exp(-jnp.abs(z)))

    sa, _ = _sigmoid_pair(_dot(h, w_ga[...]))
    sga_ref[...] = sa.astype(BF16)
    sb, _ = _sigmoid_pair(_dot(h, w_gb[...]))
    sgb_ref[...] = sb.astype(BF16)


def _const_spec(shape):
    nd = len(shape)
    return pl.BlockSpec(shape, lambda *_: (0,) * nd)


def _inproj(x2, ln0_g, ln0_b, hg_lb, fb_pad, ws, tm):
    t = x2.shape[0]
    tok = lambda w: pl.BlockSpec((tm, w), lambda i: (i, 0))
    out_shapes = [
        jax.ShapeDtypeStruct((t, HG_WIDTH), BF16),
        jax.ShapeDtypeStruct((t, HG_WIDTH), F32),
        jax.ShapeDtypeStruct((t, HG_WIDTH), BF16),
        jax.ShapeDtypeStruct((t, HG_WIDTH), BF16),
        jax.ShapeDtypeStruct((t, HG_WIDTH), BF16),
        jax.ShapeDtypeStruct((t, FOX_WIDTH), BF16),
        jax.ShapeDtypeStruct((FOX_WIDTH, t), BF16),
        jax.ShapeDtypeStruct((t, FOX_WIDTH), BF16),
        jax.ShapeDtypeStruct((t, LANES), F32),
        jax.ShapeDtypeStruct((t, D_MODEL), BF16),
        jax.ShapeDtypeStruct((t, D_MODEL), BF16),
    ]
    out_specs = [tok(HG_WIDTH)] * 5 + [
        tok(FOX_WIDTH), pl.BlockSpec((FOX_WIDTH, tm), lambda i: (0, i)), tok(FOX_WIDTH),
        tok(LANES), tok(D_MODEL), tok(D_MODEL)]
    in_specs = [tok(D_MODEL), _const_spec(ln0_g.shape), _const_spec(ln0_b.shape),
                _const_spec(hg_lb.shape), _const_spec(fb_pad.shape)]
    in_specs += [_const_spec(w.shape) for w in ws]
    return pl.pallas_call(
        _inproj_kernel,
        grid=(t // tm,),
        in_specs=in_specs,
        out_specs=out_specs,
        out_shape=out_shapes,
        compiler_params=pltpu.CompilerParams(
            dimension_semantics=("parallel",), vmem_limit_bytes=VMEM_LIMIT),
        name="inproj",
    )(x2, ln0_g, ln0_b, hg_lb, fb_pad, *ws)


def _foxcum_kernel(lf_ref, c_ref, acc_ref, *, rows):
    s = lf_ref.shape[1]
    ri = lax.broadcasted_iota(jnp.int32, (rows, rows), 0)
    ci = lax.broadcasted_iota(jnp.int32, (rows, rows), 1)
    tril = jnp.where(ci <= ri, 1.0, 0.0).astype(BF16)
    carry = jnp.zeros((1, LANES), F32)
    for i in range(s // rows):
        hi, mid, lo = _split3(lf_ref[0, i * rows:(i + 1) * rows, :])
        c = _dot(tril, hi) + _dot(tril, mid) + _dot(tril, lo) + carry
        acc_ref[i * rows:(i + 1) * rows, :] = c
        carry = c[rows - 1:rows, :]
    c_ref[0] = acc_ref[...].T[0:FOX_HEADS, :]


def _foxcum(lf3):
    b, s, _ = lf3.shape
    return pl.pallas_call(
        functools.partial(_foxcum_kernel, rows=256),
        grid=(b,),
        in_specs=[pl.BlockSpec((1, s, LANES), lambda i: (i, 0, 0))],
        out_specs=pl.BlockSpec((1, FOX_HEADS, s), lambda i: (i, 0, 0)),
        out_shape=jax.ShapeDtypeStruct((b, FOX_HEADS, s), F32),
        scratch_shapes=[pltpu.VMEM((s, LANES), F32)],
        compiler_params=pltpu.CompilerParams(dimension_semantics=("parallel",)),
        name="foxcum",
    )(lf3)


def _hgrn_decay_matrix():
    c = HG_CHUNK
    w = np.zeros((2 * HG_LEVELS + 2, c, c), np.float32)
    j = np.arange(c)
    for p in range(HG_LEVELS):
        n = 1 << p
        for t in range(c):
            m = (t >> (p + 1) << (p + 1)) + n - 1
            if t & n:
                w[2 * p, t] = (j > m) & (j <= t)
            else:
                w[2 * p + 1, t] = (j > t) & (j <= m)
    for t in range(c):
        w[2 * HG_LEVELS, t] = j <= t
        w[2 * HG_LEVELS + 1, t] = j > t
    return w.reshape(-1, c)


def _hgrn_kernel(w_ref, q_ref, g_ref, k_ref, v_ref, og_ref, ng_ref, y_ref, st_ref, *, nc):
    c = HG_CHUNK

    @pl.when(pl.program_id(2) == 0)
    def _():
        st_ref[...] = jnp.zeros_like(st_ref)

    def chunks_on_lanes(x):
        return jnp.concatenate([x[i * c:(i + 1) * c, :] for i in range(nc)], axis=1)

    def lane_chunk(x, i):
        return x[:, i * LANES:(i + 1) * LANES]

    g_hi, g_mid, g_lo = _split3(chunks_on_lanes(g_ref[0]))
    q = chunks_on_lanes(q_ref[0].astype(F32))
    k = chunks_on_lanes(k_ref[0].astype(F32))
    v = v_ref[0]

    def decay(block):
        w = w_ref[block * c:(block + 1) * c, :]
        return jnp.exp(_dot(w, g_hi) + _dot(w, g_mid) + _dot(w, g_lo))

    ti = lax.broadcasted_iota(jnp.int32, (c, c), 0)
    si = lax.broadcasted_iota(jnp.int32, (c, c), 1)
    x = jnp.bitwise_xor(ti, si)
    level = jnp.zeros((c, c), jnp.int32)
    for p in range(HG_LEVELS):
        level = level + jnp.where(x >= (1 << p), 1, 0)
    level = jnp.where(ti >= si, level, -1)

    qb = q.astype(BF16)
    kb = k.astype(BF16)
    scores = [jnp.where(level == 0, _dot_nt(lane_chunk(qb, i), lane_chunk(kb, i)), 0.0)
              for i in range(nc)]
    for p in range(HG_LEVELS):
        qt = (q * decay(2 * p)).astype(BF16)
        kt = (k * decay(2 * p + 1)).astype(BF16)
        for i in range(nc):
            a = _dot_nt(lane_chunk(qt, i), lane_chunk(kt, i))
            scores[i] = jnp.where(level == p + 1, a, scores[i])

    e_start = decay(2 * HG_LEVELS)
    q_in = (q * e_start).astype(BF16)
    k_out = (k * decay(2 * HG_LEVELS + 1)).astype(BF16)
    ng = ng_ref[...]

    st = st_ref[...]
    for i in range(nc):
        v_i = v[i * c:(i + 1) * c, :]
        o = _dot(scores[i].astype(BF16), v_i) + _dot_nt(lane_chunk(q_in, i), st.astype(BF16))
        st = st * lane_chunk(e_start, i)[c - 1:c, :] + _dot_tn(v_i, lane_chunk(k_out, i))
        o = o * lax.rsqrt(jnp.mean(o * o, axis=-1, keepdims=True) + RMS_EPS)
        gate = og_ref[0, i * c:(i + 1) * c, :].astype(F32)
        y_ref[0, i * c:(i + 1) * c, :] = (o * ng * gate).astype(BF16)
    st_ref[...] = st


def _hgrn(qa, g, kk, v, og, norm_g, tc):
    b, s, _ = qa.shape
    nc = tc // HG_CHUNK
    w = jnp.asarray(_hgrn_decay_matrix(), BF16)
    tok = pl.BlockSpec((1, tc, HG_KDIM), lambda bi, hi, si: (bi, si, hi))
    return pl.pallas_call(
        functools.partial(_hgrn_kernel, nc=nc),
        grid=(b, HG_HEADS, s // tc),
        in_specs=[_const_spec(w.shape), tok, tok, tok, tok, tok,
                  pl.BlockSpec((1, HG_KDIM), lambda bi, hi, si: (0, hi))],
        out_specs=tok,
        out_shape=jax.ShapeDtypeStruct((b, s, HG_WIDTH), BF16),
        scratch_shapes=[pltpu.VMEM((HG_KDIM, HG_KDIM), F32)],
        compiler_params=pltpu.CompilerParams(
            dimension_semantics=("parallel", "parallel", "arbitrary"),
            vmem_limit_bytes=VMEM_LIMIT),
        name="hgrn",
    )(w, qa, g, kk, v, og, norm_g)


def _fox_kernel(q_ref, kt_ref, v_ref, c_ref, o_ref, *, tq):
    qi = pl.program_id(2)
    q = q_ref[0]
    lane = lax.broadcasted_iota(jnp.int32, (tq, LANES), 1)
    first = lane < FOX_HDIM
    zero = jnp.zeros_like(q)
    qs = (jnp.where(first, q, zero), jnp.where(first, zero, q))
    q0 = pl.multiple_of(qi * tq, tq)
    c_q = [c_ref[0, 0, hh:hh + 1, pl.ds(q0, LANES)][:, 0:1] for hh in range(2)]

    def block(j, carry, diagonal):
        k0 = pl.multiple_of(j * tq, tq)
        kt = kt_ref[:, pl.ds(k0, tq)]
        v = v_ref[0, pl.ds(k0, tq), :]
        out = []
        for hh in range(2):
            m, l, acc = carry[hh]
            s = _dot(qs[hh], kt) + (c_q[hh] - c_ref[0, 0, hh:hh + 1, pl.ds(k0, tq)])
            if diagonal:
                ri = lax.broadcasted_iota(jnp.int32, (tq, tq), 0)
                ci = lax.broadcasted_iota(jnp.int32, (tq, tq), 1)
                s = jnp.where(ci <= ri, s, NEG_BIG)
            m_new = jnp.maximum(m, jnp.max(s, axis=-1, keepdims=True))
            alpha = jnp.exp(m - m_new)
            p = jnp.exp(s - m_new)
            l = alpha * l + jnp.sum(p, axis=-1, keepdims=True)
            acc = alpha * acc + _dot(p.astype(BF16), v)
            out.append((m_new, l, acc))
        return tuple(out)

    init = tuple((jnp.full((tq, 1), NEG_BIG, F32), jnp.zeros((tq, 1), F32),
                  jnp.zeros((tq, LANES), F32)) for _ in range(2))
    carry = lax.fori_loop(0, qi, lambda j, cr: block(j, cr, False), init)
    carry = block(qi, carry, True)
    o0 = carry[0][2] / carry[0][1]
    o1 = carry[1][2] / carry[1][1]
    o_ref[0] = jnp.where(first, o0, o1).astype(BF16)


def _fox(fq, fkt, fv, c4, tq):
    b, s, _ = fq.shape
    pairs = FOX_HEADS // 2
    return pl.pallas_call(
        functools.partial(_fox_kernel, tq=tq),
        grid=(b, pairs, s // tq),
        in_specs=[
            pl.BlockSpec((1, tq, LANES), lambda bi, hp, qi: (bi, qi, hp)),
            pl.BlockSpec((LANES, s), lambda bi, hp, qi: (hp, bi)),
            pl.BlockSpec((1, s, LANES), lambda bi, hp, qi: (bi, 0, hp)),
            pl.BlockSpec((1, 1, 2, s), lambda bi, hp, qi: (bi, hp, 0, 0)),
        ],
        out_specs=pl.BlockSpec((1, tq, LANES), lambda bi, hp, qi: (bi, qi, hp)),
        out_shape=jax.ShapeDtypeStruct((b, s, FOX_WIDTH), BF16),
        compiler_params=pltpu.CompilerParams(
            dimension_semantics=("parallel", "parallel", "arbitrary"),
            vmem_limit_bytes=VMEM_LIMIT),
        name="fox",
    )(fq, fkt, fv, c4)


def _merge_kernel(x_ref, g0_ref, b0_ref, ya_ref, yb_ref, sga_ref, sgb_ref,
                  wa_ref, wb_ref, wo_ref, g1_ref, b1_ref, h1_ref, *, alpha):
    h0 = _layer_norm(x_ref[...], g0_ref[...], b0_ref[...])
    merged = (sga_ref[...].astype(F32) * _dot(ya_ref[...], wa_ref[...])
              + sgb_ref[...].astype(F32) * _dot(yb_ref[...], wb_ref[...]))
    mix = _dot(merged.astype(BF16), wo_ref[...])
    h1_ref[...] = _layer_norm(alpha * h0 + mix, g1_ref[...], b1_ref[...])


def _merge(x2, ln0_g, ln0_b, ya, yb, sga, sgb, wa, wb, wo, ln1_g, ln1_b, alpha, tm):
    t = x2.shape[0]
    tok = lambda w: pl.BlockSpec((tm, w), lambda i: (i, 0))
    consts = [ln0_g, ln0_b]
    tail = [wa, wb, wo, ln1_g, ln1_b]
    return pl.pallas_call(
        functools.partial(_merge_kernel, alpha=alpha),
        grid=(t // tm,),
        in_specs=[tok(D_MODEL)] + [_const_spec(a.shape) for a in consts]
        + [tok(HG_WIDTH), tok(FOX_WIDTH), tok(D_MODEL), tok(D_MODEL)]
        + [_const_spec(a.shape) for a in tail],
        out_specs=tok(D_MODEL),
        out_shape=jax.ShapeDtypeStruct((t, D_MODEL), F32),
        compiler_params=pltpu.CompilerParams(
            dimension_semantics=("parallel",), vmem_limit_bytes=VMEM_LIMIT),
        name="merge",
    )(x2, ln0_g, ln0_b, ya, yb, sga, sgb, wa, wb, wo, ln1_g, ln1_b)


def _ffn_kernel(h_ref, p_ref, w1_ref, w2_ref, wpg_ref, wp_ref, g2_ref, b2_ref, o_ref,
                *, alpha, fc):
    h = h_ref[...]
    hb = h.astype(BF16)
    sg, _ = _sigmoid_pair(_dot(hb, wpg_ref[...]))
    acc = alpha * h + sg * _dot(p_ref[...].astype(BF16), wp_ref[...])
    for f in range(0, D_FF, fc):
        u = jnp.maximum(_dot(hb, w1_ref[:, f:f + fc]), 0.0)
        acc = acc + _dot((u * u).astype(BF16), w2_ref[f:f + fc, :])
    o_ref[...] = _layer_norm(acc, g2_ref[...], b2_ref[...])


def _ffn(h1, p2, w1, w2, wpg, wp, ln2_g, ln2_b, alpha, tm):
    t = h1.shape[0]
    tok = lambda w: pl.BlockSpec((tm, w), lambda i: (i, 0))
    consts = [w1, w2, wpg, wp, ln2_g, ln2_b]
    return pl.pallas_call(
        functools.partial(_ffn_kernel, alpha=alpha, fc=1024),
        grid=(t // tm,),
        in_specs=[tok(D_MODEL), tok(PLE_DIM)] + [_const_spec(a.shape) for a in consts],
        out_specs=tok(D_MODEL),
        out_shape=jax.ShapeDtypeStruct((t, D_MODEL), F32),
        compiler_params=pltpu.CompilerParams(
            dimension_semantics=("parallel",), vmem_limit_bytes=VMEM_LIMIT),
        name="ffn",
    )(h1, p2, w1, w2, wpg, wp, ln2_g, ln2_b)


def kernel(x, p, ln0_g, ln0_b, w_in, hg_lb, hg_norm_g, fox_fb, w_a, w_b, w_o,
           ln1_g, ln1_b, w_ff1, w_ff2, w_pg, w_p, ln2_g, ln2_b):
    b, s, d = x.shape
    depth = w_in.shape[0]
    assert depth == 1 and d == D_MODEL and hg_lb.shape[0] == 2
    assert s % 512 == 0
    t = b * s
    alpha = (2.0 * depth) ** 0.25
    row = lambda a: a.reshape(1, -1).astype(F32)

    sizes = [HG_WIDTH] * 4 + [FOX_WIDTH] * 3 + [FOX_HEADS] + [D_MODEL] * 2
    offs = np.concatenate([[0], np.cumsum(sizes)])
    wi = w_in[0]
    parts = [wi[:, offs[i]:offs[i + 1]] for i in range(len(sizes))]
    w_hq, w_hf, w_hi, w_hg, w_fq, w_fk, w_fv, w_ff, w_ga, w_gb = parts
    w_ff = jnp.pad(w_ff, ((0, 0), (0, LANES - FOX_HEADS)))
    fb_pad = jnp.pad(fox_fb[0].astype(F32), (0, LANES - FOX_HEADS)).reshape(1, LANES)
    ws = [w.astype(BF16) for w in
          (w_hq, w_hf, w_hi, w_hg, w_fq, w_fk.T, w_fv, w_ff, w_ga, w_gb)]

    x2 = x.reshape(t, d)
    g0, b0 = row(ln0_g), row(ln0_b)
    (qa, g, kk, v, og, fq, fkt, fv, lf, sga, sgb) = _inproj(
        x2, g0, b0, hg_lb.astype(F32), fb_pad, ws, tm=512)

    c = _foxcum(lf.reshape(b, s, LANES))
    c4 = c.reshape(b, FOX_HEADS // 2, 2, s)

    r3 = lambda a: a.reshape(b, s, a.shape[-1])
    ya = _hgrn(r3(qa), r3(g), r3(kk), r3(v), r3(og), hg_norm_g.astype(F32), tc=512)
    yb = _fox(r3(fq), fkt, r3(fv), c4, tq=256)

    h1 = _merge(x2, g0, b0, ya.reshape(t, HG_WIDTH), yb.reshape(t, FOX_WIDTH), sga, sgb,
                w_a[0].astype(BF16), w_b[0].astype(BF16), w_o[0].astype(BF16),
                row(ln1_g[0]), row(ln1_b[0]), alpha, tm=512)
    out = _ffn(h1, p[0].reshape(t, PLE_DIM), w_ff1[0].astype(BF16), w_ff2[0].astype(BF16),
               w_pg[0].astype(BF16), w_p[0].astype(BF16), row(ln2_g[0]), row(ln2_b[0]),
               alpha, tm=512)
    return out.reshape(b, s, d)
```

```python
import functools

import numpy as np
import jax
import jax.numpy as jnp
from jax import lax
from jax.experimental import pallas as pl
from jax.experimental.pallas import tpu as pltpu

F32 = jnp.float32
BF16 = jnp.bfloat16

D_MODEL = 1024
HG_WIDTH = 512
HG_KDIM = 128
HG_HEADS = HG_WIDTH // HG_KDIM
FOX_WIDTH = 512
FOX_HDIM = 64
FOX_HEADS = FOX_WIDTH // FOX_HDIM
D_FF = 4 * D_MODEL
PLE_DIM = 256
LN_EPS = 1e-5
RMS_EPS = 1e-6
LANES = 128
HG_CHUNK = 64
HG_LEVELS = 6
NEG_BIG = -1e30
LOG2E = 1.4426950408889634
VMEM_LIMIT = 56 * 1024 * 1024


def _layer_norm(x, g, b):
    mu = jnp.mean(x, axis=-1, keepdims=True)
    xc = x - mu
    var = jnp.mean(xc * xc, axis=-1, keepdims=True)
    return xc * lax.rsqrt(var + LN_EPS) * g + b


def _sigmoid_pair(x):
    e = jnp.exp(-jnp.abs(x))
    r = 1.0 / (1.0 + e)
    er = e * r
    pos = x >= 0
    return jnp.where(pos, r, er), jnp.where(pos, er, r)


def _split3(x):
    hi = x.astype(BF16)
    r = x - hi.astype(F32)
    mid = r.astype(BF16)
    lo = (r - mid.astype(F32)).astype(BF16)
    return hi, mid, lo


def _dot(a, b):
    return jnp.dot(a, b, preferred_element_type=F32)


def _dot_nt(a, b):
    return lax.dot_general(a, b, (((1,), (1,)), ((), ())), preferred_element_type=F32)


def _dot_tn(a, b):
    return lax.dot_general(a, b, (((0,), (0,)), ((), ())), preferred_element_type=F32)


def _inproj_kernel(x_ref, g0_ref, b0_ref, lb_ref, fb_ref,
                   w_hq, w_hf, w_hi, w_hg, w_fqt, w_fk, w_fvt, w_ff, w_ga, w_gb,
                   qa_ref, g_ref, kk_ref, v_ref, og_ref,
                   fqt_ref, fk_ref, fvt_ref, lf_ref, sga_ref, sgb_ref):
    h = _layer_norm(x_ref[...], g0_ref[...], b0_ref[...]).astype(BF16)

    lbl = lb_ref[...]
    e = jnp.exp(lbl - jnp.max(lbl, axis=0, keepdims=True))
    lb = e[0:1, :] / jnp.sum(e, axis=0, keepdims=True)

    q = _dot(h, w_hq[...])
    sq, _ = _sigmoid_pair(q)
    qa_ref[...] = (q * sq).astype(BF16)

    sf, snf = _sigmoid_pair(_dot(h, w_hf[...]))
    g_ref[...] = jnp.log(lb + (1.0 - lb) * sf)
    kk_ref[...] = ((1.0 - lb) * snf).astype(BF16)

    v_ref[...] = _dot(h, w_hi[...]).astype(BF16)

    og = _dot(h, w_hg[...])
    so, _ = _sigmoid_pair(og)
    og_ref[...] = (og * so).astype(BF16)

    fqt_ref[...] = (_dot_nt(w_fqt[...], h) * (FOX_HDIM ** -0.5 * LOG2E)).astype(BF16)
    fk_ref[...] = _dot(h, w_fk[...]).astype(BF16)
    fvt_ref[...] = _dot_nt(w_fvt[...], h).astype(BF16)

    z = _dot(h, w_ff[...]) + fb_ref[...]
    lf_ref[...] = jnp.minimum(z, 0.0) - jnp.log(1.0 + jnp.exp(-jnp.abs(z)))

    sa, _ = _sigmoid_pair(_dot(h, w_ga[...]))
    sga_ref[...] = sa.astype(BF16)
    sb, _ = _sigmoid_pair(_dot(h, w_gb[...]))
    sgb_ref[...] = sb.astype(BF16)


def _const_spec(shape):
    nd = len(shape)
    return pl.BlockSpec(shape, lambda *_: (0,) * nd)


def _inproj(x2, ln0_g, ln0_b, hg_lb, fb_pad, ws, tm):
    t = x2.shape[0]
    tok = lambda w: pl.BlockSpec((tm, w), lambda i: (i, 0))
    out_shapes = [
        jax.ShapeDtypeStruct((t, HG_WIDTH), BF16),
        jax.ShapeDtypeStruct((t, HG_WIDTH), F32),
        jax.ShapeDtypeStruct((t, HG_WIDTH), BF16),
        jax.ShapeDtypeStruct((t, HG_WIDTH), BF16),
        jax.ShapeDtypeStruct((t, HG_WIDTH), BF16),
        jax.ShapeDtypeStruct((FOX_WIDTH, t), BF16),
        jax.ShapeDtypeStruct((t, FOX_WIDTH), BF16),
        jax.ShapeDtypeStruct((FOX_WIDTH, t), BF16),
        jax.ShapeDtypeStruct((t, LANES), F32),
        jax.ShapeDtypeStruct((t, D_MODEL), BF16),
        jax.ShapeDtypeStruct((t, D_MODEL), BF16),
    ]
    tok_t = pl.BlockSpec((FOX_WIDTH, tm), lambda i: (0, i))
    out_specs = [tok(HG_WIDTH)] * 5 + [
        tok_t, tok(FOX_WIDTH), tok_t, tok(LANES), tok(D_MODEL), tok(D_MODEL)]
    in_specs = [tok(D_MODEL), _const_spec(ln0_g.shape), _const_spec(ln0_b.shape),
                _const_spec(hg_lb.shape), _const_spec(fb_pad.shape)]
    in_specs += [_const_spec(w.shape) for w in ws]
    return pl.pallas_call(
        _inproj_kernel,
        grid=(t // tm,),
        in_specs=in_specs,
        out_specs=out_specs,
        out_shape=out_shapes,
        compiler_params=pltpu.CompilerParams(
            dimension_semantics=("parallel",), vmem_limit_bytes=VMEM_LIMIT),
        name="inproj",
    )(x2, ln0_g, ln0_b, hg_lb, fb_pad, *ws)


def _foxcum_kernel(lf_ref, ccol_ref, crow_ref, *, rows):
    s = lf_ref.shape[1]
    ri = lax.broadcasted_iota(jnp.int32, (rows, rows), 0)
    ci = lax.broadcasted_iota(jnp.int32, (rows, rows), 1)
    tril = jnp.where(ci <= ri, 1.0, 0.0).astype(BF16)
    carry = jnp.zeros((1, LANES), F32)
    for i in range(s // rows):
        hi, mid, lo = _split3(lf_ref[0, i * rows:(i + 1) * rows, :])
        c = _dot(tril, hi) + _dot(tril, mid) + _dot(tril, lo) + carry
        ccol_ref[0, i * rows:(i + 1) * rows, :] = c * LOG2E
        carry = c[rows - 1:rows, :]
    crow_ref[0] = ccol_ref[0].T[0:FOX_HEADS, :]


def _foxcum(lf3):
    b, s, _ = lf3.shape
    return pl.pallas_call(
        functools.partial(_foxcum_kernel, rows=256),
        grid=(b,),
        in_specs=[pl.BlockSpec((1, s, LANES), lambda i: (i, 0, 0))],
        out_specs=[pl.BlockSpec((1, s, LANES), lambda i: (i, 0, 0)),
                   pl.BlockSpec((1, FOX_HEADS, s), lambda i: (i, 0, 0))],
        out_shape=[jax.ShapeDtypeStruct((b, s, LANES), F32),
                   jax.ShapeDtypeStruct((b, FOX_HEADS, s), F32)],
        compiler_params=pltpu.CompilerParams(dimension_semantics=("parallel",)),
        name="foxcum",
    )(lf3)


def _hgrn_decay_matrix():
    c = HG_CHUNK
    w = np.zeros((HG_LEVELS + 2, c, c), np.float32)
    j = np.arange(c)
    for p in range(HG_LEVELS):
        n = 1 << p
        for t in range(c):
            start = t - t % n
            if t & n:
                w[p, t] = (j >= start) & (j <= t)
            else:
                w[p, t] = (j > t) & (j <= start + n - 1)
    for t in range(c):
        w[HG_LEVELS, t] = j <= t
        w[HG_LEVELS + 1, t] = j > t
    return np.tile(w.reshape(-1, c), (1, 3))


def _hgrn_kernel(w_ref, q_ref, g_ref, k_ref, v_ref, og_ref, ng_ref, y_ref, st_ref, d_ref,
                 *, nc):
    c = HG_CHUNK

    @pl.when(pl.program_id(2) == 0)
    def _():
        st_ref[...] = jnp.zeros_like(st_ref)

    def chunks_on_lanes(x):
        return jnp.concatenate([x[i * c:(i + 1) * c, :] for i in range(nc)], axis=1)

    def lane_chunk(x, i):
        return x[:, i * LANES:(i + 1) * LANES]

    g3 = jnp.concatenate(_split3(chunks_on_lanes(g_ref[0])), axis=0)
    d_ref[...] = _dot(w_ref[...], g3)
    q = chunks_on_lanes(q_ref[0])
    k = chunks_on_lanes(k_ref[0])
    v = v_ref[0]

    def decay(block):
        return jnp.exp(d_ref[block * c:(block + 1) * c, :])

    ti = lax.broadcasted_iota(jnp.int32, (c, c), 0)
    si = lax.broadcasted_iota(jnp.int32, (c, c), 1)
    x = jnp.bitwise_xor(ti, si)
    level = jnp.zeros((c, c), jnp.int32)
    for p in range(HG_LEVELS):
        level = level + jnp.where(x >= (1 << p), 1, 0)
    level = jnp.where(ti >= si, level, -1)

    scores = [jnp.where(level == 0, _dot_nt(lane_chunk(q, i), lane_chunk(k, i)), 0.0)
              for i in range(nc)]
    for p in range(HG_LEVELS):
        e = decay(p).astype(BF16)
        qt = q * e
        kt = k * e
        for i in range(nc):
            a = _dot_nt(lane_chunk(qt, i), lane_chunk(kt, i))
            scores[i] = jnp.where(level == p + 1, a, scores[i])

    e_start = decay(HG_LEVELS)
    q_in = q * e_start.astype(BF16)
    k_out = k * decay(HG_LEVELS + 1).astype(BF16)
    ng = ng_ref[...]

    v_chunks = [v[i * c:(i + 1) * c, :] for i in range(nc)]
    updates = [_dot_tn(v_chunks[i], lane_chunk(k_out, i)) for i in range(nc)]
    st = st_ref[...]
    states = []
    for i in range(nc):
        states.append(st.astype(BF16))
        st = st * lane_chunk(e_start, i)[c - 1:c, :] + updates[i]
    st_ref[...] = st
    for i in range(nc):
        o = (_dot(scores[i].astype(BF16), v_chunks[i])
             + _dot_nt(lane_chunk(q_in, i), states[i]))
        o = o * lax.rsqrt(jnp.mean(o * o, axis=-1, keepdims=True) + RMS_EPS)
        gate = og_ref[0, i * c:(i + 1) * c, :].astype(F32)
        y_ref[0, i * c:(i + 1) * c, :] = (o * ng * gate).astype(BF16)


def _hgrn(qa, g, kk, v, og, norm_g, tc):
    b, s, _ = qa.shape
    nc = tc // HG_CHUNK
    w = jnp.asarray(_hgrn_decay_matrix(), BF16)
    tok = pl.BlockSpec((1, tc, HG_KDIM), lambda bi, hi, si: (bi, si, hi))
    return pl.pallas_call(
        functools.partial(_hgrn_kernel, nc=nc),
        grid=(b, HG_HEADS, s // tc),
        in_specs=[_const_spec(w.shape), tok, tok, tok, tok, tok,
                  pl.BlockSpec((1, HG_KDIM), lambda bi, hi, si: (0, hi))],
        out_specs=tok,
        out_shape=jax.ShapeDtypeStruct((b, s, HG_WIDTH), BF16),
        scratch_shapes=[pltpu.VMEM((HG_KDIM, HG_KDIM), F32),
                        pltpu.VMEM(((HG_LEVELS + 2) * HG_CHUNK, nc * HG_KDIM), F32)],
        compiler_params=pltpu.CompilerParams(
            dimension_semantics=("parallel", "parallel", "arbitrary"),
            vmem_limit_bytes=VMEM_LIMIT),
        name="hgrn",
    )(w, qa, g, kk, v, og, norm_g)


def _fox_kernel(qt_ref, k_ref, vt_ref, ccol_ref, crow_ref, o_ref,
                kaug_ref, vaug_ref, m_ref, acc_ref, s_ref, mx_ref, p_ref, *, tq):
    hp = pl.program_id(1)
    qi = pl.program_id(2)
    s_len = k_ref.shape[1]
    aug_base = (FOX_HDIM, 0)
    sum_row = (FOX_HDIM, 0)

    @pl.when(qi == 0)
    def _():
        lane = lax.broadcasted_iota(jnp.int32, (s_len, LANES), 1)
        row = lax.broadcasted_iota(jnp.int32, (LANES, s_len), 0)
        pr = lax.broadcasted_iota(jnp.int32, (LANES, LANES), 0)
        pc = lax.broadcasted_iota(jnp.int32, (LANES, LANES), 1)
        k = k_ref[0].astype(F32)
        vt = vt_ref[...].astype(F32)
        c_terms = _split3(-ccol_ref[0])
        for hh in range(2):
            head = 2 * hp + hh
            base = aug_base[hh]
            bias = sum(_dot(term, jnp.where(pr == head, jnp.where(pc == base + i, 1.0, 0.0), 0.0)
                            .astype(BF16)) for i, term in enumerate(c_terms))
            rel = lane - base
            ones = jnp.where(rel >= 3, jnp.where(rel < 6, 1.0, 0.0), 0.0)
            is_own = (lane < FOX_HDIM) if hh == 0 else (lane >= FOX_HDIM)
            kaug_ref[hh] = jnp.where(is_own, k, bias + ones).astype(BF16)
            is_own_t = (row < FOX_HDIM) if hh == 0 else (row >= FOX_HDIM)
            vaug_ref[hh] = jnp.where(is_own_t, vt, jnp.where(row == sum_row[hh], 1.0, 0.0)
                                     ).astype(BF16)

    qt = qt_ref[...].astype(F32)
    row = lax.broadcasted_iota(jnp.int32, (LANES, tq), 0)
    q0 = pl.multiple_of(qi * tq, tq)
    qt_aug = []
    for hh in range(2):
        hi, mid, lo = _split3(crow_ref[0, 0, hh:hh + 1, pl.ds(q0, LANES)][:, 0:1])
        rel = row - aug_base[hh]
        aug = jnp.where(rel == 3, hi.astype(F32),
                        jnp.where(rel == 4, mid.astype(F32),
                                  jnp.where(rel == 5, lo.astype(F32),
                                            jnp.where(rel < 3, 1.0, 0.0))))
        aug = jnp.where(rel >= 0, aug, 0.0)
        is_own = (row < FOX_HDIM) if hh == 0 else (row >= FOX_HDIM)
        qt_aug.append(jnp.where(is_own, qt, aug).astype(BF16))

    m_ref[...] = jnp.full(m_ref.shape, NEG_BIG, F32)
    acc_ref[...] = jnp.zeros(acc_ref.shape, F32)

    def scores(j, slot, diagonal=False):
        k0 = pl.multiple_of(j * tq, tq)
        for hh in range(2):
            st = _dot(kaug_ref[hh, pl.ds(k0, tq), :], qt_aug[hh])
            if diagonal:
                ki = lax.broadcasted_iota(jnp.int32, (tq, tq), 0)
                ti = lax.broadcasted_iota(jnp.int32, (tq, tq), 1)
                st = jnp.where(ki <= ti, st, NEG_BIG)
            s_ref[slot, hh] = st
            mx_ref[slot, hh] = jnp.max(st, axis=0, keepdims=True)

    def softmax(slot):
        scales = []
        for hh in range(2):
            m_old = m_ref[hh]
            m_new = jnp.maximum(m_old, mx_ref[slot, hh])
            p_ref[hh] = jnp.exp2(s_ref[slot, hh] - m_new).astype(BF16)
            scales.append(jnp.exp2(m_old - m_new))
            m_ref[hh] = m_new
        return scales

    def accumulate(j, scales):
        k0 = pl.multiple_of(j * tq, tq)
        for hh in range(2):
            acc_ref[hh] = (scales[hh] * acc_ref[hh]
                           + _dot(vaug_ref[hh, :, pl.ds(k0, tq)], p_ref[hh]))

    scores(qi, 0, diagonal=True)
    accumulate(qi, softmax(0))

    def step(j, cur):
        scales = softmax(cur)
        scores(j + 1, 1 - cur)
        accumulate(j, scales)

    @pl.when(qi > 0)
    def _():
        scores(0, 0)
        n_steps = qi - 1

        def two_steps(jj, carry):
            step(2 * jj, 0)
            step(2 * jj + 1, 1)
            return carry

        lax.fori_loop(0, n_steps // 2, two_steps, 0)

        @pl.when(lax.rem(n_steps, 2) == 1)
        def _():
            step(n_steps - 1, 0)
            accumulate(n_steps, softmax(1))

        @pl.when(lax.rem(n_steps, 2) == 0)
        def _():
            accumulate(n_steps, softmax(0))

    acc0 = acc_ref[0]
    acc1 = acc_ref[1]
    ot = jnp.where(row < FOX_HDIM, acc0 / acc0[sum_row[0]:sum_row[0] + 1, :],
                   acc1 / acc1[sum_row[1]:sum_row[1] + 1, :])
    o_ref[0] = ot.T.astype(BF16)


def _fox(fqt, fk, fvt, ccol, crow4, tq):
    b, s, _ = fk.shape
    pairs = FOX_HEADS // 2
    nq = s // tq
    return pl.pallas_call(
        functools.partial(_fox_kernel, tq=tq),
        grid=(b, pairs, nq),
        in_specs=[
            pl.BlockSpec((LANES, tq), lambda bi, hp, qi: (hp, bi * nq + qi)),
            pl.BlockSpec((1, s, LANES), lambda bi, hp, qi: (bi, 0, hp)),
            pl.BlockSpec((LANES, s), lambda bi, hp, qi: (hp, bi)),
            pl.BlockSpec((1, s, LANES), lambda bi, hp, qi: (bi, 0, 0)),
            pl.BlockSpec((1, 1, 2, s), lambda bi, hp, qi: (bi, hp, 0, 0)),
        ],
        out_specs=pl.BlockSpec((1, tq, LANES), lambda bi, hp, qi: (bi, qi, hp)),
        out_shape=jax.ShapeDtypeStruct((b, s, FOX_WIDTH), BF16),
        scratch_shapes=[pltpu.VMEM((2, s, LANES), BF16), pltpu.VMEM((2, LANES, s), BF16),
                        pltpu.VMEM((2, 1, tq), F32), pltpu.VMEM((2, LANES, tq), F32),
                        pltpu.VMEM((2, 2, tq, tq), F32), pltpu.VMEM((2, 2, 1, tq), F32),
                        pltpu.VMEM((2, tq, tq), BF16)],
        compiler_params=pltpu.CompilerParams(
            dimension_semantics=("parallel", "parallel", "arbitrary"),
            vmem_limit_bytes=VMEM_LIMIT),
        name="fox",
    )(fqt, fk, fvt, ccol, crow4)


def _merge_kernel(x_ref, g0_ref, b0_ref, ya_ref, yb_ref, sga_ref, sgb_ref,
                  wa_ref, wb_ref, wo_ref, g1_ref, b1_ref, h1_ref, *, alpha):
    h0 = _layer_norm(x_ref[...], g0_ref[...], b0_ref[...])
    merged = (sga_ref[...].astype(F32) * _dot(ya_ref[...], wa_ref[...])
              + sgb_ref[...].astype(F32) * _dot(yb_ref[...], wb_ref[...]))
    mix = _dot(merged.astype(BF16), wo_ref[...])
    h1_ref[...] = _layer_norm(alpha * h0 + mix, g1_ref[...], b1_ref[...])


def _merge(x2, ln0_g, ln0_b, ya, yb, sga, sgb, wa, wb, wo, ln1_g, ln1_b, alpha, tm):
    t = x2.shape[0]
    tok = lambda w: pl.BlockSpec((tm, w), lambda i: (i, 0))
    consts = [ln0_g, ln0_b]
    tail = [wa, wb, wo, ln1_g, ln1_b]
    return pl.pallas_call(
        functools.partial(_merge_kernel, alpha=alpha),
        grid=(t // tm,),
        in_specs=[tok(D_MODEL)] + [_const_spec(a.shape) for a in consts]
        + [tok(HG_WIDTH), tok(FOX_WIDTH), tok(D_MODEL), tok(D_MODEL)]
        + [_const_spec(a.shape) for a in tail],
        out_specs=tok(D_MODEL),
        out_shape=jax.ShapeDtypeStruct((t, D_MODEL), F32),
        compiler_params=pltpu.CompilerParams(
            dimension_semantics=("parallel",), vmem_limit_bytes=VMEM_LIMIT),
        name="merge",
    )(x2, ln0_g, ln0_b, ya, yb, sga, sgb, wa, wb, wo, ln1_g, ln1_b)


def _ffn_kernel(h_ref, p_ref, w1_ref, w2_ref, wpg_ref, wp_ref, g2_ref, b2_ref, o_ref,
                *, alpha, fc):
    h = h_ref[...]
    hb = h.astype(BF16)
    sg, _ = _sigmoid_pair(_dot(hb, wpg_ref[...]))
    acc = alpha * h + sg * _dot(p_ref[...].astype(BF16), wp_ref[...])
    for f in range(0, D_FF, fc):
        u = jnp.maximum(_dot(hb, w1_ref[:, f:f + fc]), 0.0)
        acc = acc + _dot((u * u).astype(BF16), w2_ref[f:f + fc, :])
    o_ref[...] = _layer_norm(acc, g2_ref[...], b2_ref[...])


def _ffn(h1, p2, w1, w2, wpg, wp, ln2_g, ln2_b, alpha, tm):
    t = h1.shape[0]
    tok = lambda w: pl.BlockSpec((tm, w), lambda i: (i, 0))
    consts = [w1, w2, wpg, wp, ln2_g, ln2_b]
    return pl.pallas_call(
        functools.partial(_ffn_kernel, alpha=alpha, fc=1024),
        grid=(t // tm,),
        in_specs=[tok(D_MODEL), tok(PLE_DIM)] + [_const_spec(a.shape) for a in consts],
        out_specs=tok(D_MODEL),
        out_shape=jax.ShapeDtypeStruct((t, D_MODEL), F32),
        compiler_params=pltpu.CompilerParams(
            dimension_semantics=("parallel",), vmem_limit_bytes=VMEM_LIMIT),
        name="ffn",
    )(h1, p2, w1, w2, wpg, wp, ln2_g, ln2_b)


def kernel(x, p, ln0_g, ln0_b, w_in, hg_lb, hg_norm_g, fox_fb, w_a, w_b, w_o,
           ln1_g, ln1_b, w_ff1, w_ff2, w_pg, w_p, ln2_g, ln2_b):
    b, s, d = x.shape
    depth = w_in.shape[0]
    assert depth == 1 and d == D_MODEL and hg_lb.shape[0] == 2
    assert s % 512 == 0
    t = b * s
    alpha = (2.0 * depth) ** 0.25
    row = lambda a: a.reshape(1, -1).astype(F32)

    sizes = [HG_WIDTH] * 4 + [FOX_WIDTH] * 3 + [FOX_HEADS] + [D_MODEL] * 2
    offs = np.concatenate([[0], np.cumsum(sizes)])
    wi = w_in[0]
    parts = [wi[:, offs[i]:offs[i + 1]] for i in range(len(sizes))]
    w_hq, w_hf, w_hi, w_hg, w_fq, w_fk, w_fv, w_ff, w_ga, w_gb = parts
    w_ff = jnp.pad(w_ff, ((0, 0), (0, LANES - FOX_HEADS)))
    fb_pad = jnp.pad(fox_fb[0].astype(F32), (0, LANES - FOX_HEADS)).reshape(1, LANES)
    ws = [w.astype(BF16) for w in
          (w_hq, w_hf, w_hi, w_hg, w_fq.T, w_fk, w_fv.T, w_ff, w_ga, w_gb)]

    x2 = x.reshape(t, d)
    g0, b0 = row(ln0_g), row(ln0_b)
    (qa, g, kk, v, og, fqt, fk, fvt, lf, sga, sgb) = _inproj(
        x2, g0, b0, hg_lb.astype(F32), fb_pad, ws, tm=512)

    ccol, crow = _foxcum(lf.reshape(b, s, LANES))
    crow4 = crow.reshape(b, FOX_HEADS // 2, 2, s)

    r3 = lambda a: a.reshape(b, s, a.shape[-1])
    ya = _hgrn(r3(qa), r3(g), r3(kk), r3(v), r3(og), hg_norm_g.astype(F32), tc=512)
    yb = _fox(fqt, r3(fk), fvt, ccol, crow4, tq=512)

    h1 = _merge(x2, g0, b0, ya.reshape(t, HG_WIDTH), yb.reshape(t, FOX_WIDTH), sga, sgb,
                w_a[0].astype(BF16), w_b[0].astype(BF16), w_o[0].astype(BF16),
                row(ln1_g[0]), row(ln1_b[0]), alpha, tm=512)
    out = _ffn(h1, p[0].reshape(t, PLE_DIM), w_ff1[0].astype(BF16), w_ff2[0].astype(BF16),
               w_pg[0].astype(BF16), w_p[0].astype(BF16), row(ln2_g[0]), row(ln2_b[0]),
               alpha, tm=512)
    return out.reshape(b, s, d)
```

```python
import functools

import numpy as np
import jax
import jax.numpy as jnp
from jax import lax
from jax.experimental import pallas as pl
from jax.experimental.pallas import tpu as pltpu

F32 = jnp.float32
BF16 = jnp.bfloat16

D_MODEL = 1024
HG_WIDTH = 512
HG_KDIM = 128
HG_HEADS = HG_WIDTH // HG_KDIM
FOX_WIDTH = 512
FOX_HDIM = 64
FOX_HEADS = FOX_WIDTH // FOX_HDIM
D_FF = 4 * D_MODEL
PLE_DIM = 256
LN_EPS = 1e-5
RMS_EPS = 1e-6
LANES = 128
HG_CHUNK = 64
HG_LEVELS = 6
NEG_BIG = -1e30
LOG2E = 1.4426950408889634
VMEM_LIMIT = 56 * 1024 * 1024
ROW_GROUPS = 2


def _layer_norm(x, g, b):
    mu = jnp.mean(x, axis=-1, keepdims=True)
    xc = x - mu
    var = jnp.mean(xc * xc, axis=-1, keepdims=True)
    return xc * lax.rsqrt(var + LN_EPS) * g + b


def _half_tanh(x):
    return 0.5 * jnp.tanh(0.5 * x)


def _split3(x):
    hi = x.astype(BF16)
    r = x - hi.astype(F32)
    mid = r.astype(BF16)
    lo = (r - mid.astype(F32)).astype(BF16)
    return hi, mid, lo


def _dot(a, b):
    return jnp.dot(a, b, preferred_element_type=F32)


def _dot_nt(a, b):
    return lax.dot_general(a, b, (((1,), (1,)), ((), ())), preferred_element_type=F32)


def _dot_tn(a, b):
    return lax.dot_general(a, b, (((0,), (0,)), ((), ())), preferred_element_type=F32)


def _inproj_kernel(x_ref, g0_ref, b0_ref, lb_ref, fb_ref,
                   w_hq, w_hf, w_hi, w_hg, w_fqt, w_fk, w_fvt, w_ff, w_ga, w_gb,
                   qa_ref, g_ref, kk_ref, v_ref, og_ref,
                   fqt_ref, fk_ref, fvt_ref, lf_ref, sga_ref, sgb_ref):
    lbl = lb_ref[...]
    e = jnp.exp(lbl - jnp.max(lbl, axis=0, keepdims=True))
    lb = e[0:1, :] / jnp.sum(e, axis=0, keepdims=True)

    tm = x_ref.shape[0]
    rows_per_group = tm // ROW_GROUPS
    for r in range(ROW_GROUPS):
        rows = slice(r * rows_per_group, (r + 1) * rows_per_group)
        h = _layer_norm(x_ref[rows, :], g0_ref[...], b0_ref[...]).astype(BF16)

        q = _dot(h, w_hq[...])
        qa_ref[rows, :] = (q * (0.5 + _half_tanh(q))).astype(BF16)

        tf = _half_tanh(_dot(h, w_hf[...]))
        g_ref[rows, :] = jnp.log(lb + (1.0 - lb) * (0.5 + tf))
        kk_ref[rows, :] = ((1.0 - lb) * (0.5 - tf)).astype(BF16)

        v_ref[rows, :] = _dot(h, w_hi[...]).astype(BF16)

        og = _dot(h, w_hg[...])
        og_ref[rows, :] = (og * (0.5 + _half_tanh(og))).astype(BF16)

        fqt_ref[:, rows] = (_dot_nt(w_fqt[...], h) * (FOX_HDIM ** -0.5 * LOG2E)).astype(BF16)
        fk_ref[rows, :] = _dot(h, w_fk[...]).astype(BF16)
        fvt_ref[:, rows] = _dot_nt(w_fvt[...], h).astype(BF16)

        z = _dot(h, w_ff[...]) + fb_ref[...]
        lf_ref[rows, :] = jnp.minimum(z, 0.0) - jnp.log(1.0 + jnp.exp(-jnp.abs(z)))

        sga_ref[rows, :] = (0.5 + _half_tanh(_dot(h, w_ga[...]))).astype(BF16)
        sgb_ref[rows, :] = (0.5 + _half_tanh(_dot(h, w_gb[...]))).astype(BF16)


def _const_spec(shape):
    nd = len(shape)
    return pl.BlockSpec(shape, lambda *_: (0,) * nd, pipeline_mode=pl.Buffered(1))


def _inproj(x2, ln0_g, ln0_b, hg_lb, fb_pad, ws, tm):
    t = x2.shape[0]
    tok = lambda w: pl.BlockSpec((tm, w), lambda i: (i, 0))
    out_shapes = [
        jax.ShapeDtypeStruct((t, HG_WIDTH), BF16),
        jax.ShapeDtypeStruct((t, HG_WIDTH), F32),
        jax.ShapeDtypeStruct((t, HG_WIDTH), BF16),
        jax.ShapeDtypeStruct((t, HG_WIDTH), BF16),
        jax.ShapeDtypeStruct((t, HG_WIDTH), BF16),
        jax.ShapeDtypeStruct((FOX_WIDTH, t), BF16),
        jax.ShapeDtypeStruct((t, FOX_WIDTH), BF16),
        jax.ShapeDtypeStruct((FOX_WIDTH, t), BF16),
        jax.ShapeDtypeStruct((t, LANES), F32),
        jax.ShapeDtypeStruct((t, D_MODEL), BF16),
        jax.ShapeDtypeStruct((t, D_MODEL), BF16),
    ]
    tok_t = pl.BlockSpec((FOX_WIDTH, tm), lambda i: (0, i))
    out_specs = [tok(HG_WIDTH)] * 5 + [
        tok_t, tok(FOX_WIDTH), tok_t, tok(LANES), tok(D_MODEL), tok(D_MODEL)]
    in_specs = [tok(D_MODEL), _const_spec(ln0_g.shape), _const_spec(ln0_b.shape),
                _const_spec(hg_lb.shape), _const_spec(fb_pad.shape)]
    in_specs += [_const_spec(w.shape) for w in ws]
    return pl.pallas_call(
        _inproj_kernel,
        grid=(t // tm,),
        in_specs=in_specs,
        out_specs=out_specs,
        out_shape=out_shapes,
        compiler_params=pltpu.CompilerParams(
            dimension_semantics=("parallel",), vmem_limit_bytes=VMEM_LIMIT),
        name="inproj",
    )(x2, ln0_g, ln0_b, hg_lb, fb_pad, *ws)


def _foxcum_kernel(lf_ref, ccol_ref, crow_ref, *, rows):
    s = lf_ref.shape[1]
    ri = lax.broadcasted_iota(jnp.int32, (rows, rows), 0)
    ci = lax.broadcasted_iota(jnp.int32, (rows, rows), 1)
    tril = jnp.where(ci <= ri, 1.0, 0.0).astype(BF16)
    carry = jnp.zeros((1, LANES), F32)
    for i in range(s // rows):
        hi, mid, lo = _split3(lf_ref[0, i * rows:(i + 1) * rows, :])
        c = _dot(tril, hi) + _dot(tril, mid) + _dot(tril, lo) + carry
        ccol_ref[0, i * rows:(i + 1) * rows, :] = c * LOG2E
        carry = c[rows - 1:rows, :]
    crow_ref[0] = ccol_ref[0].T[0:FOX_HEADS, :]


def _foxcum(lf3):
    b, s, _ = lf3.shape
    return pl.pallas_call(
        functools.partial(_foxcum_kernel, rows=256),
        grid=(b,),
        in_specs=[pl.BlockSpec((1, s, LANES), lambda i: (i, 0, 0))],
        out_specs=[pl.BlockSpec((1, s, LANES), lambda i: (i, 0, 0)),
                   pl.BlockSpec((1, FOX_HEADS, s), lambda i: (i, 0, 0))],
        out_shape=[jax.ShapeDtypeStruct((b, s, LANES), F32),
                   jax.ShapeDtypeStruct((b, FOX_HEADS, s), F32)],
        compiler_params=pltpu.CompilerParams(dimension_semantics=("parallel",)),
        name="foxcum",
    )(lf3)


def _hgrn_decay_matrix():
    c = HG_CHUNK
    w = np.zeros((HG_LEVELS + 2, c, c), np.float32)
    j = np.arange(c)
    for p in range(HG_LEVELS):
        n = 1 << p
        for t in range(c):
            start = t - t % n
            if t & n:
                w[p, t] = (j >= start) & (j <= t)
            else:
                w[p, t] = (j > t) & (j <= start + n - 1)
    for t in range(c):
        w[HG_LEVELS, t] = j <= t
        w[HG_LEVELS + 1, t] = j > t
    return np.tile(w.reshape(-1, c), (1, 3))


def _hgrn_kernel(w_ref, q_ref, g_ref, k_ref, v_ref, og_ref, ng_ref, y_ref, st_ref, d_ref,
                 *, nc):
    c = HG_CHUNK

    @pl.when(pl.program_id(2) == 0)
    def _():
        st_ref[...] = jnp.zeros_like(st_ref)

    def chunks_on_lanes(x):
        return jnp.concatenate([x[i * c:(i + 1) * c, :] for i in range(nc)], axis=1)

    def lane_chunk(x, i):
        return x[:, i * LANES:(i + 1) * LANES]

    g3 = jnp.concatenate(_split3(chunks_on_lanes(g_ref[0])), axis=0)
    d_ref[...] = _dot(w_ref[...], g3)
    q = chunks_on_lanes(q_ref[0])
    k = chunks_on_lanes(k_ref[0])
    v = v_ref[0]

    def decay(block):
        return jnp.exp(d_ref[block * c:(block + 1) * c, :])

    ti = lax.broadcasted_iota(jnp.int32, (c, c), 0)
    si = lax.broadcasted_iota(jnp.int32, (c, c), 1)
    x = jnp.bitwise_xor(ti, si)
    level = jnp.zeros((c, c), jnp.int32)
    for p in range(HG_LEVELS):
        level = level + jnp.where(x >= (1 << p), 1, 0)
    level = jnp.where(ti >= si, level, -1)

    scores = [jnp.where(level == 0, _dot_nt(lane_chunk(q, i), lane_chunk(k, i)), 0.0)
              for i in range(nc)]
    for p in range(HG_LEVELS):
        e = decay(p).astype(BF16)
        qt = q * e
        kt = k * e
        for i in range(nc):
            a = _dot_nt(lane_chunk(qt, i), lane_chunk(kt, i))
            scores[i] = jnp.where(level == p + 1, a, scores[i])

    e_start = decay(HG_LEVELS)
    q_in = q * e_start.astype(BF16)
    k_out = k * decay(HG_LEVELS + 1).astype(BF16)
    ng = ng_ref[...]

    v_chunks = [v[i * c:(i + 1) * c, :] for i in range(nc)]
    updates = [_dot_tn(v_chunks[i], lane_chunk(k_out, i)) for i in range(nc)]
    st = st_ref[...]
    states = []
    for i in range(nc):
        states.append(st.astype(BF16))
        st = st * lane_chunk(e_start, i)[c - 1:c, :] + updates[i]
    st_ref[...] = st
    for i in range(nc):
        o = (_dot(scores[i].astype(BF16), v_chunks[i])
             + _dot_nt(lane_chunk(q_in, i), states[i]))
        o = o * lax.rsqrt(jnp.mean(o * o, axis=-1, keepdims=True) + RMS_EPS)
        gate = og_ref[0, i * c:(i + 1) * c, :].astype(F32)
        y_ref[0, i * c:(i + 1) * c, :] = (o * ng * gate).astype(BF16)


def _hgrn(qa, g, kk, v, og, norm_g, tc):
    b, s, _ = qa.shape
    nc = tc // HG_CHUNK
    w = jnp.asarray(_hgrn_decay_matrix(), BF16)
    tok = pl.BlockSpec((1, tc, HG_KDIM), lambda bi, hi, si: (bi, si, hi))
    return pl.pallas_call(
        functools.partial(_hgrn_kernel, nc=nc),
        grid=(b, HG_HEADS, s // tc),
        in_specs=[_const_spec(w.shape), tok, tok, tok, tok, tok,
                  pl.BlockSpec((1, HG_KDIM), lambda bi, hi, si: (0, hi))],
        out_specs=tok,
        out_shape=jax.ShapeDtypeStruct((b, s, HG_WIDTH), BF16),
        scratch_shapes=[pltpu.VMEM((HG_KDIM, HG_KDIM), F32),
                        pltpu.VMEM(((HG_LEVELS + 2) * HG_CHUNK, nc * HG_KDIM), F32)],
        compiler_params=pltpu.CompilerParams(
            dimension_semantics=("parallel", "parallel", "arbitrary"),
            vmem_limit_bytes=VMEM_LIMIT),
        name="hgrn",
    )(w, qa, g, kk, v, og, norm_g)


def _fox_kernel(qt_ref, k_ref, vt_ref, ccol_ref, crow_ref, o_ref,
                kaug_ref, vaug_ref, m_ref, acc_ref, s_ref, mx_ref, p_ref, *, tq):
    hp = pl.program_id(1)
    qi = pl.program_id(2)
    s_len = k_ref.shape[1]
    aug_base = (FOX_HDIM, 0)
    sum_row = (FOX_HDIM, 0)

    @pl.when(qi == 0)
    def _():
        lane = lax.broadcasted_iota(jnp.int32, (s_len, LANES), 1)
        row = lax.broadcasted_iota(jnp.int32, (LANES, s_len), 0)
        pr = lax.broadcasted_iota(jnp.int32, (LANES, LANES), 0)
        pc = lax.broadcasted_iota(jnp.int32, (LANES, LANES), 1)
        k = k_ref[0].astype(F32)
        vt = vt_ref[...].astype(F32)
        c_terms = _split3(-ccol_ref[0])
        for hh in range(2):
            head = 2 * hp + hh
            base = aug_base[hh]
            bias = sum(_dot(term, jnp.where(pr == head, jnp.where(pc == base + i, 1.0, 0.0), 0.0)
                            .astype(BF16)) for i, term in enumerate(c_terms))
            rel = lane - base
            ones = jnp.where(rel >= 3, jnp.where(rel < 6, 1.0, 0.0), 0.0)
            is_own = (lane < FOX_HDIM) if hh == 0 else (lane >= FOX_HDIM)
            kaug_ref[hh] = jnp.where(is_own, k, bias + ones).astype(BF16)
            is_own_t = (row < FOX_HDIM) if hh == 0 else (row >= FOX_HDIM)
            vaug_ref[hh] = jnp.where(is_own_t, vt, jnp.where(row == sum_row[hh], 1.0, 0.0)
                                     ).astype(BF16)

    qt = qt_ref[...].astype(F32)
    row = lax.broadcasted_iota(jnp.int32, (LANES, tq), 0)
    q0 = pl.multiple_of(qi * tq, tq)
    qt_aug = []
    for hh in range(2):
        hi, mid, lo = _split3(crow_ref[0, 0, hh:hh + 1, pl.ds(q0, LANES)][:, 0:1])
        rel = row - aug_base[hh]
        aug = jnp.where(rel == 3, hi.astype(F32),
                        jnp.where(rel == 4, mid.astype(F32),
                                  jnp.where(rel == 5, lo.astype(F32),
                                            jnp.where(rel < 3, 1.0, 0.0))))
        aug = jnp.where(rel >= 0, aug, 0.0)
        is_own = (row < FOX_HDIM) if hh == 0 else (row >= FOX_HDIM)
        qt_aug.append(jnp.where(is_own, qt, aug).astype(BF16))

    m_ref[...] = jnp.full(m_ref.shape, NEG_BIG, F32)
    acc_ref[...] = jnp.zeros(acc_ref.shape, F32)

    def scores(j, slot, diagonal=False):
        k0 = pl.multiple_of(j * tq, tq)
        for hh in range(2):
            st = _dot(kaug_ref[hh, pl.ds(k0, tq), :], qt_aug[hh])
            if diagonal:
                ki = lax.broadcasted_iota(jnp.int32, (tq, tq), 0)
                ti = lax.broadcasted_iota(jnp.int32, (tq, tq), 1)
                st = jnp.where(ki <= ti, st, NEG_BIG)
            s_ref[slot, hh] = st
            mx_ref[slot, hh] = jnp.max(st, axis=0, keepdims=True)

    def softmax(slot):
        scales = []
        for hh in range(2):
            m_old = m_ref[hh]
            m_new = jnp.maximum(m_old, mx_ref[slot, hh])
            p_ref[hh] = jnp.exp2(s_ref[slot, hh] - m_new).astype(BF16)
            scales.append(jnp.exp2(m_old - m_new))
            m_ref[hh] = m_new
        return scales

    def accumulate(j, scales):
        k0 = pl.multiple_of(j * tq, tq)
        for hh in range(2):
            acc_ref[hh] = (scales[hh] * acc_ref[hh]
                           + _dot(vaug_ref[hh, :, pl.ds(k0, tq)], p_ref[hh]))

    def step(j, cur, j_next):
        half = tq // 2
        k_cur = pl.multiple_of(j * tq, tq)
        k_next = pl.multiple_of(j_next * tq, tq)
        m_new, scales = [], []
        for hh in range(2):
            m_old = m_ref[hh]
            m_new.append(jnp.maximum(m_old, mx_ref[cur, hh]))
            scales.append(jnp.exp2(m_old - m_new[hh]))
            m_ref[hh] = m_new[hh]
        pv = [None, None]
        mx = [None, None]
        for part in range(2):
            rows = slice(part * half, (part + 1) * half)
            for hh in range(2):
                st = _dot(kaug_ref[hh, pl.ds(k_next + part * half, half), :], qt_aug[hh])
                s_ref[1 - cur, hh, rows, :] = st
                part_max = jnp.max(st, axis=0, keepdims=True)
                mx[hh] = part_max if part == 0 else jnp.maximum(mx[hh], part_max)
            for hh in range(2):
                p = jnp.exp2(s_ref[cur, hh, rows, :] - m_new[hh]).astype(BF16)
                d = _dot(vaug_ref[hh, :, pl.ds(k_cur + part * half, half)], p)
                pv[hh] = d if part == 0 else pv[hh] + d
        for hh in range(2):
            acc_ref[hh] = scales[hh] * acc_ref[hh] + pv[hh]
            mx_ref[1 - cur, hh] = mx[hh]

    scores(qi, 0, diagonal=True)
    step(qi, 0, 0)

    @pl.when(qi > 0)
    def _():
        n_steps = qi - 1

        def two_steps(jj, carry):
            step(2 * jj, 1, 2 * jj + 1)
            step(2 * jj + 1, 0, 2 * jj + 2)
            return carry

        lax.fori_loop(0, n_steps // 2, two_steps, 0)

        @pl.when(lax.rem(n_steps, 2) == 1)
        def _():
            step(n_steps - 1, 1, n_steps)
            accumulate(n_steps, softmax(0))

        @pl.when(lax.rem(n_steps, 2) == 0)
        def _():
            accumulate(n_steps, softmax(1))

    acc0 = acc_ref[0]
    acc1 = acc_ref[1]
    ot = jnp.where(row < FOX_HDIM, acc0 / acc0[sum_row[0]:sum_row[0] + 1, :],
                   acc1 / acc1[sum_row[1]:sum_row[1] + 1, :])
    o_ref[0] = ot.T.astype(BF16)


def _fox(fqt, fk, fvt, ccol, crow4, tq):
    b, s, _ = fk.shape
    pairs = FOX_HEADS // 2
    nq = s // tq
    return pl.pallas_call(
        functools.partial(_fox_kernel, tq=tq),
        grid=(b, pairs, nq),
        in_specs=[
            pl.BlockSpec((LANES, tq), lambda bi, hp, qi: (hp, bi * nq + qi)),
            pl.BlockSpec((1, s, LANES), lambda bi, hp, qi: (bi, 0, hp)),
            pl.BlockSpec((LANES, s), lambda bi, hp, qi: (hp, bi)),
            pl.BlockSpec((1, s, LANES), lambda bi, hp, qi: (bi, 0, 0)),
            pl.BlockSpec((1, 1, 2, s), lambda bi, hp, qi: (bi, hp, 0, 0)),
        ],
        out_specs=pl.BlockSpec((1, tq, LANES), lambda bi, hp, qi: (bi, qi, hp)),
        out_shape=jax.ShapeDtypeStruct((b, s, FOX_WIDTH), BF16),
        scratch_shapes=[pltpu.VMEM((2, s, LANES), BF16), pltpu.VMEM((2, LANES, s), BF16),
                        pltpu.VMEM((2, 1, tq), F32), pltpu.VMEM((2, LANES, tq), F32),
                        pltpu.VMEM((2, 2, tq, tq), F32), pltpu.VMEM((2, 2, 1, tq), F32),
                        pltpu.VMEM((2, tq, tq), BF16)],
        compiler_params=pltpu.CompilerParams(
            dimension_semantics=("parallel", "parallel", "arbitrary"),
            vmem_limit_bytes=VMEM_LIMIT),
        name="fox",
    )(fqt, fk, fvt, ccol, crow4)


def _tail_kernel(x_ref, g0_ref, b0_ref, ya_ref, yb_ref, sga_ref, sgb_ref, p_ref,
                 wa_ref, wb_ref, wo_ref, g1_ref, b1_ref,
                 w1_ref, w2_ref, wpg_ref, wp_ref, g2_ref, b2_ref, o_ref, *, alpha, fc):
    rows_per_group = x_ref.shape[0] // ROW_GROUPS
    groups = [slice(r * rows_per_group, (r + 1) * rows_per_group) for r in range(ROW_GROUPS)]

    def mixer(rows):
        h0 = _layer_norm(x_ref[rows, :], g0_ref[...], b0_ref[...])
        merged = (sga_ref[rows, :].astype(F32) * _dot(ya_ref[rows, :], wa_ref[...])
                  + sgb_ref[rows, :].astype(F32) * _dot(yb_ref[rows, :], wb_ref[...]))
        mix = _dot(merged.astype(BF16), wo_ref[...])
        return _layer_norm(alpha * h0 + mix, g1_ref[...], b1_ref[...])

    def channel_mixer(rows, h):
        hb = h.astype(BF16)
        sg = 0.5 + _half_tanh(_dot(hb, wpg_ref[...]))
        acc = alpha * h + sg * _dot(p_ref[rows, :].astype(BF16), wp_ref[...])
        for f in range(0, D_FF, fc):
            u = jnp.maximum(_dot(hb, w1_ref[:, f:f + fc]), 0.0)
            acc = acc + _dot((u * u).astype(BF16), w2_ref[f:f + fc, :])
        o_ref[rows, :] = _layer_norm(acc, g2_ref[...], b2_ref[...])

    h1 = [mixer(rows) for rows in groups]
    for rows, h in zip(groups, h1):
        channel_mixer(rows, h)


def _tail(x2, ln0_g, ln0_b, ya, yb, sga, sgb, p2, wa, wb, wo, ln1_g, ln1_b,
          w1, w2, wpg, wp, ln2_g, ln2_b, alpha, tm):
    t = x2.shape[0]
    tok = lambda w: pl.BlockSpec((tm, w), lambda i: (i, 0))
    consts = [wa, wb, wo, ln1_g, ln1_b, w1, w2, wpg, wp, ln2_g, ln2_b]
    return pl.pallas_call(
        functools.partial(_tail_kernel, alpha=alpha, fc=1024),
        grid=(t // tm,),
        in_specs=[tok(D_MODEL), _const_spec(ln0_g.shape), _const_spec(ln0_b.shape),
                  tok(HG_WIDTH), tok(FOX_WIDTH), tok(D_MODEL), tok(D_MODEL), tok(PLE_DIM)]
        + [_const_spec(a.shape) for a in consts],
        out_specs=tok(D_MODEL),
        out_shape=jax.ShapeDtypeStruct((t, D_MODEL), F32),
        compiler_params=pltpu.CompilerParams(
            dimension_semantics=("parallel",), vmem_limit_bytes=VMEM_LIMIT),
        name="tail",
    )(x2, ln0_g, ln0_b, ya, yb, sga, sgb, p2, wa, wb, wo, ln1_g, ln1_b,
      w1, w2, wpg, wp, ln2_g, ln2_b)


def kernel(x, p, ln0_g, ln0_b, w_in, hg_lb, hg_norm_g, fox_fb, w_a, w_b, w_o,
           ln1_g, ln1_b, w_ff1, w_ff2, w_pg, w_p, ln2_g, ln2_b):
    b, s, d = x.shape
    depth = w_in.shape[0]
    assert depth == 1 and d == D_MODEL and hg_lb.shape[0] == 2
    assert s % 512 == 0
    t = b * s
    alpha = (2.0 * depth) ** 0.25
    row = lambda a: a.reshape(1, -1).astype(F32)

    sizes = [HG_WIDTH] * 4 + [FOX_WIDTH] * 3 + [FOX_HEADS] + [D_MODEL] * 2
    offs = np.concatenate([[0], np.cumsum(sizes)])
    wi = w_in[0]
    parts = [wi[:, offs[i]:offs[i + 1]] for i in range(len(sizes))]
    w_hq, w_hf, w_hi, w_hg, w_fq, w_fk, w_fv, w_ff, w_ga, w_gb = parts
    w_ff = jnp.pad(w_ff, ((0, 0), (0, LANES - FOX_HEADS)))
    fb_pad = jnp.pad(fox_fb[0].astype(F32), (0, LANES - FOX_HEADS)).reshape(1, LANES)
    ws = [w.astype(BF16) for w in
          (w_hq, w_hf, w_hi, w_hg, w_fq.T, w_fk, w_fv.T, w_ff, w_ga, w_gb)]

    x2 = x.reshape(t, d)
    g0, b0 = row(ln0_g), row(ln0_b)
    (qa, g, kk, v, og, fqt, fk, fvt, lf, sga, sgb) = _inproj(
        x2, g0, b0, hg_lb.astype(F32), fb_pad, ws, tm=1024)

    ccol, crow = _foxcum(lf.reshape(b, s, LANES))
    crow4 = crow.reshape(b, FOX_HEADS // 2, 2, s)

    r3 = lambda a: a.reshape(b, s, a.shape[-1])
    ya = _hgrn(r3(qa), r3(g), r3(kk), r3(v), r3(og), hg_norm_g.astype(F32), tc=512)
    yb = _fox(fqt, r3(fk), fvt, ccol, crow4, tq=512)

    out = _tail(x2, g0, b0, ya.reshape(t, HG_WIDTH), yb.reshape(t, FOX_WIDTH), sga, sgb,
                p[0].reshape(t, PLE_DIM),
                w_a[0].astype(BF16), w_b[0].astype(BF16), w_o[0].astype(BF16),
                row(ln1_g[0]), row(ln1_b[0]),
                w_ff1[0].astype(BF16), w_ff2[0].astype(BF16),
                w_pg[0].astype(BF16), w_p[0].astype(BF16), row(ln2_g[0]), row(ln2_b[0]),
                alpha, tm=512)
    return out.reshape(b, s, d)
```

```python
import functools

import numpy as np
import jax
import jax.numpy as jnp
from jax import lax
from jax.experimental import pallas as pl
from jax.experimental.pallas import tpu as pltpu

F32 = jnp.float32
BF16 = jnp.bfloat16

D_MODEL = 1024
HG_WIDTH = 512
HG_KDIM = 128
HG_HEADS = HG_WIDTH // HG_KDIM
FOX_WIDTH = 512
FOX_HDIM = 64
FOX_HEADS = FOX_WIDTH // FOX_HDIM
D_FF = 4 * D_MODEL
PLE_DIM = 256
LN_EPS = 1e-5
RMS_EPS = 1e-6
LANES = 128
HG_CHUNK = 64
HG_LEVELS = 6
NEG_BIG = -1e30
LOG2E = 1.4426950408889634
VMEM_LIMIT = 56 * 1024 * 1024
ROW_GROUPS = 2


def _layer_norm(x, g, b):
    mu = jnp.mean(x, axis=-1, keepdims=True)
    xc = x - mu
    var = jnp.mean(xc * xc, axis=-1, keepdims=True)
    return xc * lax.rsqrt(var + LN_EPS) * g + b


def _half_tanh(x):
    return 0.5 * jnp.tanh(0.5 * x)


def _split3(x):
    hi = x.astype(BF16)
    r = x - hi.astype(F32)
    mid = r.astype(BF16)
    lo = (r - mid.astype(F32)).astype(BF16)
    return hi, mid, lo


def _dot(a, b):
    return jnp.dot(a, b, preferred_element_type=F32)


def _dot_nt(a, b):
    return lax.dot_general(a, b, (((1,), (1,)), ((), ())), preferred_element_type=F32)


def _dot_tn(a, b):
    return lax.dot_general(a, b, (((0,), (0,)), ((), ())), preferred_element_type=F32)


def _inproj_kernel(x_ref, g0_ref, b0_ref, lb_ref, fb_ref,
                   w_hq, w_hf, w_hi, w_hg, w_fqt, w_fk, w_fvt, w_ff, w_ga, w_gb,
                   qa_ref, g_ref, kk_ref, v_ref, og_ref,
                   fqt_ref, fk_ref, fvt_ref, lf_ref, sga_ref, sgb_ref):
    lbl = lb_ref[...]
    e = jnp.exp(lbl - jnp.max(lbl, axis=0, keepdims=True))
    lb = e[0:1, :] / jnp.sum(e, axis=0, keepdims=True)

    tm = x_ref.shape[0]
    rows_per_group = tm // ROW_GROUPS
    for r in range(ROW_GROUPS):
        rows = slice(r * rows_per_group, (r + 1) * rows_per_group)
        h = _layer_norm(x_ref[rows, :], g0_ref[...], b0_ref[...]).astype(BF16)

        q = _dot(h, w_hq[...])
        qa_ref[rows, :] = (q * (0.5 + _half_tanh(q))).astype(BF16)

        tf = _half_tanh(_dot(h, w_hf[...]))
        g_ref[rows, :] = jnp.log(lb + (1.0 - lb) * (0.5 + tf))
        kk_ref[rows, :] = ((1.0 - lb) * (0.5 - tf)).astype(BF16)

        v_ref[rows, :] = _dot(h, w_hi[...]).astype(BF16)

        og = _dot(h, w_hg[...])
        og_ref[rows, :] = (og * (0.5 + _half_tanh(og))).astype(BF16)

        fqt_ref[:, rows] = (_dot_nt(w_fqt[...], h) * (FOX_HDIM ** -0.5 * LOG2E)).astype(BF16)
        fk_ref[rows, :] = _dot(h, w_fk[...]).astype(BF16)
        fvt_ref[:, rows] = _dot_nt(w_fvt[...], h).astype(BF16)

        z = _dot(h, w_ff[...]) + fb_ref[...]
        lf_ref[rows, :] = jnp.minimum(z, 0.0) - jnp.log(1.0 + jnp.exp(-jnp.abs(z)))

        sga_ref[rows, :] = (0.5 + _half_tanh(_dot(h, w_ga[...]))).astype(BF16)
        sgb_ref[rows, :] = (0.5 + _half_tanh(_dot(h, w_gb[...]))).astype(BF16)


def _const_spec(shape):
    nd = len(shape)
    return pl.BlockSpec(shape, lambda *_: (0,) * nd, pipeline_mode=pl.Buffered(1))


def _inproj(x2, ln0_g, ln0_b, hg_lb, fb_pad, ws, tm):
    t = x2.shape[0]
    tok = lambda w: pl.BlockSpec((tm, w), lambda i: (i, 0))
    out_shapes = [
        jax.ShapeDtypeStruct((t, HG_WIDTH), BF16),
        jax.ShapeDtypeStruct((t, HG_WIDTH), F32),
        jax.ShapeDtypeStruct((t, HG_WIDTH), BF16),
        jax.ShapeDtypeStruct((t, HG_WIDTH), BF16),
        jax.ShapeDtypeStruct((t, HG_WIDTH), BF16),
        jax.ShapeDtypeStruct((FOX_WIDTH, t), BF16),
        jax.ShapeDtypeStruct((t, FOX_WIDTH), BF16),
        jax.ShapeDtypeStruct((FOX_WIDTH, t), BF16),
        jax.ShapeDtypeStruct((t, LANES), F32),
        jax.ShapeDtypeStruct((t, D_MODEL), BF16),
        jax.ShapeDtypeStruct((t, D_MODEL), BF16),
    ]
    tok_t = pl.BlockSpec((FOX_WIDTH, tm), lambda i: (0, i))
    out_specs = [tok(HG_WIDTH)] * 5 + [
        tok_t, tok(FOX_WIDTH), tok_t, tok(LANES), tok(D_MODEL), tok(D_MODEL)]
    in_specs = [tok(D_MODEL), _const_spec(ln0_g.shape), _const_spec(ln0_b.shape),
                _const_spec(hg_lb.shape), _const_spec(fb_pad.shape)]
    in_specs += [_const_spec(w.shape) for w in ws]
    return pl.pallas_call(
        _inproj_kernel,
        grid=(t // tm,),
        in_specs=in_specs,
        out_specs=out_specs,
        out_shape=out_shapes,
        compiler_params=pltpu.CompilerParams(
            dimension_semantics=("parallel",), vmem_limit_bytes=VMEM_LIMIT),
        name="inproj",
    )(x2, ln0_g, ln0_b, hg_lb, fb_pad, *ws)


FOX_BIAS_BASE = (FOX_HDIM, 0)


def _fox_bias_routing():
    pairs = FOX_HEADS // 2
    route = np.zeros((3 * LANES, pairs * LANES), np.float32)
    ones = np.zeros((1, pairs * LANES), np.float32)
    for head in range(FOX_HEADS):
        base = (head // 2) * LANES + FOX_BIAS_BASE[head % 2]
        for i in range(3):
            route[i * LANES + head, base + i] = 1.0
            ones[0, base + 3 + i] = 1.0
    return route, ones


def _foxcum_kernel(lf_ref, route_ref, ones_ref, kbias_ref, crow_ref, ccol_ref, *, rows):
    s = lf_ref.shape[1]
    ri = lax.broadcasted_iota(jnp.int32, (rows, rows), 0)
    ci = lax.broadcasted_iota(jnp.int32, (rows, rows), 1)
    tril = jnp.where(ci <= ri, 1.0, 0.0).astype(BF16)
    carry = jnp.zeros((1, LANES), F32)
    for i in range(s // rows):
        hi, mid, lo = _split3(lf_ref[0, i * rows:(i + 1) * rows, :])
        c = _dot(tril, hi) + _dot(tril, mid) + _dot(tril, lo) + carry
        c2 = c * LOG2E
        ccol_ref[i * rows:(i + 1) * rows, :] = c2
        terms = jnp.concatenate(_split3(-c2), axis=1)
        kbias_ref[0, i * rows:(i + 1) * rows, :] = (
            _dot(terms, route_ref[...]) + ones_ref[...]).astype(BF16)
        carry = c[rows - 1:rows, :]
    crow_ref[0] = ccol_ref[...].T[0:FOX_HEADS, :]


def _foxcum(lf3):
    b, s, _ = lf3.shape
    route, ones = _fox_bias_routing()
    route = jnp.asarray(route, BF16)
    ones = jnp.asarray(ones, F32)
    width = route.shape[1]
    return pl.pallas_call(
        functools.partial(_foxcum_kernel, rows=256),
        grid=(b,),
        in_specs=[pl.BlockSpec((1, s, LANES), lambda i: (i, 0, 0)),
                  _const_spec(route.shape), _const_spec(ones.shape)],
        out_specs=[pl.BlockSpec((1, s, width), lambda i: (i, 0, 0)),
                   pl.BlockSpec((1, FOX_HEADS, s), lambda i: (i, 0, 0))],
        out_shape=[jax.ShapeDtypeStruct((b, s, width), BF16),
                   jax.ShapeDtypeStruct((b, FOX_HEADS, s), F32)],
        scratch_shapes=[pltpu.VMEM((s, LANES), F32)],
        compiler_params=pltpu.CompilerParams(dimension_semantics=("parallel",)),
        name="foxcum",
    )(lf3, route, ones)


def _hgrn_decay_matrix():
    c = HG_CHUNK
    w = np.zeros((HG_LEVELS + 2, c, c), np.float32)
    j = np.arange(c)
    for p in range(HG_LEVELS):
        n = 1 << p
        for t in range(c):
            start = t - t % n
            if t & n:
                w[p, t] = (j >= start) & (j <= t)
            else:
                w[p, t] = (j > t) & (j <= start + n - 1)
    for t in range(c):
        w[HG_LEVELS, t] = j <= t
        w[HG_LEVELS + 1, t] = j > t
    return np.tile(w.reshape(-1, c), (1, 3))


def _hgrn_kernel(w_ref, q_ref, g_ref, k_ref, v_ref, og_ref, ng_ref, y_ref, st_ref, d_ref,
                 *, nc):
    c = HG_CHUNK

    @pl.when(pl.program_id(2) == 0)
    def _():
        st_ref[...] = jnp.zeros_like(st_ref)

    def chunks_on_lanes(x):
        return jnp.concatenate([x[i * c:(i + 1) * c, :] for i in range(nc)], axis=1)

    def lane_chunk(x, i):
        return x[:, i * LANES:(i + 1) * LANES]

    g3 = jnp.concatenate(_split3(chunks_on_lanes(g_ref[0])), axis=0)
    d_ref[...] = _dot(w_ref[...], g3)
    q = chunks_on_lanes(q_ref[0])
    k = chunks_on_lanes(k_ref[0])
    v = v_ref[0]

    def decay(block):
        return jnp.exp(d_ref[block * c:(block + 1) * c, :])

    ti = lax.broadcasted_iota(jnp.int32, (c, c), 0)
    si = lax.broadcasted_iota(jnp.int32, (c, c), 1)
    x = jnp.bitwise_xor(ti, si)
    level = jnp.zeros((c, c), jnp.int32)
    for p in range(HG_LEVELS):
        level = level + jnp.where(x >= (1 << p), 1, 0)
    level = jnp.where(ti >= si, level, -1)

    scores = [jnp.where(level == 0, _dot_nt(lane_chunk(q, i), lane_chunk(k, i)), 0.0)
              for i in range(nc)]
    for p in range(HG_LEVELS):
        e = decay(p).astype(BF16)
        qt = q * e
        kt = k * e
        for i in range(nc):
            a = _dot_nt(lane_chunk(qt, i), lane_chunk(kt, i))
            scores[i] = jnp.where(level == p + 1, a, scores[i])

    e_start = decay(HG_LEVELS)
    q_in = q * e_start.astype(BF16)
    k_out = k * decay(HG_LEVELS + 1).astype(BF16)
    ng = ng_ref[...]

    v_chunks = [v[i * c:(i + 1) * c, :] for i in range(nc)]
    updates = [_dot_tn(v_chunks[i], lane_chunk(k_out, i)) for i in range(nc)]
    st = st_ref[...]
    states = []
    for i in range(nc):
        states.append(st.astype(BF16))
        st = st * lane_chunk(e_start, i)[c - 1:c, :] + updates[i]
    st_ref[...] = st
    for i in range(nc):
        o = (_dot(scores[i].astype(BF16), v_chunks[i])
             + _dot_nt(lane_chunk(q_in, i), states[i]))
        o = o * lax.rsqrt(jnp.mean(o * o, axis=-1, keepdims=True) + RMS_EPS)
        gate = og_ref[0, i * c:(i + 1) * c, :].astype(F32)
        y_ref[0, i * c:(i + 1) * c, :] = (o * ng * gate).astype(BF16)


def _hgrn(qa, g, kk, v, og, norm_g, tc):
    b, s, _ = qa.shape
    nc = tc // HG_CHUNK
    w = jnp.asarray(_hgrn_decay_matrix(), BF16)
    tok = pl.BlockSpec((1, tc, HG_KDIM), lambda bi, hi, si: (bi, si, hi))
    return pl.pallas_call(
        functools.partial(_hgrn_kernel, nc=nc),
        grid=(b, HG_HEADS, s // tc),
        in_specs=[_const_spec(w.shape), tok, tok, tok, tok, tok,
                  pl.BlockSpec((1, HG_KDIM), lambda bi, hi, si: (0, hi))],
        out_specs=tok,
        out_shape=jax.ShapeDtypeStruct((b, s, HG_WIDTH), BF16),
        scratch_shapes=[pltpu.VMEM((HG_KDIM, HG_KDIM), F32),
                        pltpu.VMEM(((HG_LEVELS + 2) * HG_CHUNK, nc * HG_KDIM), F32)],
        compiler_params=pltpu.CompilerParams(
            dimension_semantics=("parallel", "parallel", "arbitrary"),
            vmem_limit_bytes=VMEM_LIMIT),
        name="hgrn",
    )(w, qa, g, kk, v, og, norm_g)


def _fox_kernel(qt_ref, k_ref, vt_ref, kbias_ref, crow_ref, o_ref,
                kaug_ref, vaug_ref, qaug_ref, m_ref, acc_ref, s_ref, mx_ref, *, tq):
    s_len = k_ref.shape[1]
    nq = s_len // tq
    half = tq // 2
    aug_base = FOX_BIAS_BASE
    sum_row = (FOX_HDIM, 0)

    def build_keys_and_values():
        lane = lax.broadcasted_iota(jnp.int32, (s_len, LANES), 1)
        row = lax.broadcasted_iota(jnp.int32, (LANES, s_len), 0)
        k = k_ref[0]
        kbias = kbias_ref[0]
        vt = vt_ref[...].astype(F32)
        for hh in range(2):
            is_own = (lane < FOX_HDIM) if hh == 0 else (lane >= FOX_HDIM)
            kaug_ref[hh] = jnp.where(is_own, k, kbias)
            is_own_t = (row < FOX_HDIM) if hh == 0 else (row >= FOX_HDIM)
            vaug_ref[hh] = jnp.where(is_own_t, vt, jnp.where(row == sum_row[hh], 1.0, 0.0)
                                     ).astype(BF16)

    row = lax.broadcasted_iota(jnp.int32, (LANES, tq), 0)

    def build_queries(qi):
        qt = qt_ref[:, qi * tq:(qi + 1) * tq].astype(F32)
        for hh in range(2):
            hi, mid, lo = _split3(crow_ref[0, 0, hh:hh + 1, qi * tq:qi * tq + LANES][:, 0:1])
            rel = row - aug_base[hh]
            aug = jnp.where(rel == 3, hi.astype(F32),
                            jnp.where(rel == 4, mid.astype(F32),
                                      jnp.where(rel == 5, lo.astype(F32),
                                                jnp.where(rel < 3, 1.0, 0.0))))
            aug = jnp.where(rel >= 0, aug, 0.0)
            is_own = (row < FOX_HDIM) if hh == 0 else (row >= FOX_HDIM)
            qaug_ref[qi, hh] = jnp.where(is_own, qt, aug).astype(BF16)

    build_keys_and_values()
    for qi in range(nq):
        build_queries(qi)
    m_ref[...] = jnp.full(m_ref.shape, NEG_BIG, F32)
    acc_ref[...] = jnp.zeros(acc_ref.shape, F32)

    def issue_scores(nxt, slot, part, mx):
        qi_next, k_next, diagonal = nxt
        rows = slice(part * half, (part + 1) * half)
        for hh in range(2):
            st = _dot(kaug_ref[hh, pl.ds(k_next + part * half, half), :], qaug_ref[qi_next, hh])
            if diagonal:
                ki = lax.broadcasted_iota(jnp.int32, (half, tq), 0) + part * half
                ti = lax.broadcasted_iota(jnp.int32, (half, tq), 1)
                st = jnp.where(ki <= ti, st, NEG_BIG)
            s_ref[slot, hh, rows, :] = st
            part_max = jnp.max(st, axis=0, keepdims=True)
            mx[hh] = part_max if part == 0 else jnp.maximum(mx[hh], part_max)

    def step(qi_cur, k_cur, cur, nxt):
        m_new, scales = [], []
        for hh in range(2):
            m_old = m_ref[qi_cur, hh]
            m_new.append(jnp.maximum(m_old, mx_ref[cur, hh]))
            scales.append(jnp.exp2(m_old - m_new[hh]))
            m_ref[qi_cur, hh] = m_new[hh]
        pv = [None, None]
        mx = [None, None]
        for part in range(2):
            rows = slice(part * half, (part + 1) * half)
            if nxt is not None:
                issue_scores(nxt, 1 - cur, part, mx)
            for hh in range(2):
                p = jnp.exp2(s_ref[cur, hh, rows, :] - m_new[hh]).astype(BF16)
                d = _dot(vaug_ref[hh, :, pl.ds(k_cur + part * half, half)], p)
                pv[hh] = d if part == 0 else pv[hh] + d
        for hh in range(2):
            acc_ref[qi_cur, hh] = scales[hh] * acc_ref[qi_cur, hh] + pv[hh]
            if nxt is not None:
                mx_ref[1 - cur, hh] = mx[hh]

    def finish(qi):
        acc0 = acc_ref[qi, 0]
        acc1 = acc_ref[qi, 1]
        ot = jnp.where(row < FOX_HDIM, acc0 / acc0[sum_row[0]:sum_row[0] + 1, :],
                       acc1 / acc1[sum_row[1]:sum_row[1] + 1, :])
        o_ref[0, qi * tq:(qi + 1) * tq, :] = ot.T.astype(BF16)

    def diagonal_of(qi):
        return (qi, qi * tq, True) if qi < nq else None

    mx0 = [None, None]
    for part in range(2):
        issue_scores(diagonal_of(0), 0, part, mx0)
    for hh in range(2):
        mx_ref[0, hh] = mx0[hh]

    slot = 0
    for qi in range(nq):
        after = diagonal_of(qi + 1)
        step(qi, qi * tq, slot, (qi, 0, False) if qi > 0 else after)
        slot = 1 - slot
        if qi > 0:
            n_inner = qi - 1
            if n_inner // 2 > 0:
                first = slot

                def two_steps(jj, carry, qi=qi, first=first):
                    k0 = pl.multiple_of(2 * jj * tq, tq)
                    step(qi, k0, first, (qi, k0 + tq, False))
                    step(qi, k0 + tq, 1 - first, (qi, k0 + 2 * tq, False))
                    return carry

                lax.fori_loop(0, n_inner // 2, two_steps, 0)
            if n_inner % 2 == 1:
                step(qi, (n_inner - 1) * tq, slot, (qi, n_inner * tq, False))
                slot = 1 - slot
            step(qi, n_inner * tq, slot, after)
            slot = 1 - slot
        finish(qi)


def _fox(fqt, fk, fvt, kbias, crow4, tq):
    b, s, _ = fk.shape
    pairs = FOX_HEADS // 2
    nq = s // tq
    return pl.pallas_call(
        functools.partial(_fox_kernel, tq=tq),
        grid=(b, pairs),
        in_specs=[
            pl.BlockSpec((LANES, s), lambda bi, hp: (hp, bi)),
            pl.BlockSpec((1, s, LANES), lambda bi, hp: (bi, 0, hp)),
            pl.BlockSpec((LANES, s), lambda bi, hp: (hp, bi)),
            pl.BlockSpec((1, s, LANES), lambda bi, hp: (bi, 0, hp)),
            pl.BlockSpec((1, 1, 2, s), lambda bi, hp: (bi, hp, 0, 0)),
        ],
        out_specs=pl.BlockSpec((1, s, LANES), lambda bi, hp: (bi, 0, hp)),
        out_shape=jax.ShapeDtypeStruct((b, s, FOX_WIDTH), BF16),
        scratch_shapes=[pltpu.VMEM((2, s, LANES), BF16), pltpu.VMEM((2, LANES, s), BF16),
                        pltpu.VMEM((nq, 2, LANES, tq), BF16),
                        pltpu.VMEM((nq, 2, 1, tq), F32), pltpu.VMEM((nq, 2, LANES, tq), F32),
                        pltpu.VMEM((2, 2, tq, tq), F32), pltpu.VMEM((2, 2, 1, tq), F32)],
        compiler_params=pltpu.CompilerParams(
            dimension_semantics=("parallel", "parallel"),
            vmem_limit_bytes=VMEM_LIMIT),
        name="fox",
    )(fqt, fk, fvt, kbias, crow4)


def _tail_kernel(x_ref, g0_ref, b0_ref, ya_ref, yb_ref, sga_ref, sgb_ref, p_ref,
                 wa_ref, wb_ref, wo_ref, g1_ref, b1_ref,
                 w1_ref, w2_ref, wpg_ref, wp_ref, g2_ref, b2_ref, o_ref, *, alpha, fc):
    rows_per_group = x_ref.shape[0] // ROW_GROUPS
    groups = [slice(r * rows_per_group, (r + 1) * rows_per_group) for r in range(ROW_GROUPS)]

    def mixer(rows):
        h0 = _layer_norm(x_ref[rows, :], g0_ref[...], b0_ref[...])
        merged = (sga_ref[rows, :].astype(F32) * _dot(ya_ref[rows, :], wa_ref[...])
                  + sgb_ref[rows, :].astype(F32) * _dot(yb_ref[rows, :], wb_ref[...]))
        mix = _dot(merged.astype(BF16), wo_ref[...])
        return _layer_norm(alpha * h0 + mix, g1_ref[...], b1_ref[...])

    def channel_mixer(rows, h):
        hb = h.astype(BF16)
        sg = 0.5 + _half_tanh(_dot(hb, wpg_ref[...]))
        acc = alpha * h + sg * _dot(p_ref[rows, :].astype(BF16), wp_ref[...])
        for f in range(0, D_FF, fc):
            u = jnp.maximum(_dot(hb, w1_ref[:, f:f + fc]), 0.0)
            acc = acc + _dot((u * u).astype(BF16), w2_ref[f:f + fc, :])
        o_ref[rows, :] = _layer_norm(acc, g2_ref[...], b2_ref[...])

    h1 = [mixer(rows) for rows in groups]
    for rows, h in zip(groups, h1):
        channel_mixer(rows, h)


def _tail(x2, ln0_g, ln0_b, ya, yb, sga, sgb, p2, wa, wb, wo, ln1_g, ln1_b,
          w1, w2, wpg, wp, ln2_g, ln2_b, alpha, tm):
    t = x2.shape[0]
    tok = lambda w: pl.BlockSpec((tm, w), lambda i: (i, 0))
    consts = [wa, wb, wo, ln1_g, ln1_b, w1, w2, wpg, wp, ln2_g, ln2_b]
    return pl.pallas_call(
        functools.partial(_tail_kernel, alpha=alpha, fc=1024),
        grid=(t // tm,),
        in_specs=[tok(D_MODEL), _const_spec(ln0_g.shape), _const_spec(ln0_b.shape),
                  tok(HG_WIDTH), tok(FOX_WIDTH), tok(D_MODEL), tok(D_MODEL), tok(PLE_DIM)]
        + [_const_spec(a.shape) for a in consts],
        out_specs=tok(D_MODEL),
        out_shape=jax.ShapeDtypeStruct((t, D_MODEL), F32),
        compiler_params=pltpu.CompilerParams(
            dimension_semantics=("parallel",), vmem_limit_bytes=VMEM_LIMIT),
        name="tail",
    )(x2, ln0_g, ln0_b, ya, yb, sga, sgb, p2, wa, wb, wo, ln1_g, ln1_b,
      w1, w2, wpg, wp, ln2_g, ln2_b)


def kernel(x, p, ln0_g, ln0_b, w_in, hg_lb, hg_norm_g, fox_fb, w_a, w_b, w_o,
           ln1_g, ln1_b, w_ff1, w_ff2, w_pg, w_p, ln2_g, ln2_b):
    b, s, d = x.shape
    depth = w_in.shape[0]
    assert depth == 1 and d == D_MODEL and hg_lb.shape[0] == 2
    assert s % 512 == 0
    t = b * s
    alpha = (2.0 * depth) ** 0.25
    row = lambda a: a.reshape(1, -1).astype(F32)

    sizes = [HG_WIDTH] * 4 + [FOX_WIDTH] * 3 + [FOX_HEADS] + [D_MODEL] * 2
    offs = np.concatenate([[0], np.cumsum(sizes)])
    wi = w_in[0]
    parts = [wi[:, offs[i]:offs[i + 1]] for i in range(len(sizes))]
    w_hq, w_hf, w_hi, w_hg, w_fq, w_fk, w_fv, w_ff, w_ga, w_gb = parts
    w_ff = jnp.pad(w_ff, ((0, 0), (0, LANES - FOX_HEADS)))
    fb_pad = jnp.pad(fox_fb[0].astype(F32), (0, LANES - FOX_HEADS)).reshape(1, LANES)
    ws = [w.astype(BF16) for w in
          (w_hq, w_hf, w_hi, w_hg, w_fq.T, w_fk, w_fv.T, w_ff, w_ga, w_gb)]

    x2 = x.reshape(t, d)
    g0, b0 = row(ln0_g), row(ln0_b)
    (qa, g, kk, v, og, fqt, fk, fvt, lf, sga, sgb) = _inproj(
        x2, g0, b0, hg_lb.astype(F32), fb_pad, ws, tm=1024)

    kbias, crow = _foxcum(lf.reshape(b, s, LANES))
    crow4 = crow.reshape(b, FOX_HEADS // 2, 2, s)

    r3 = lambda a: a.reshape(b, s, a.shape[-1])
    ya = _hgrn(r3(qa), r3(g), r3(kk), r3(v), r3(og), hg_norm_g.astype(F32), tc=512)
    yb = _fox(fqt, r3(fk), fvt, kbias, crow4, tq=512)

    out = _tail(x2, g0, b0, ya.reshape(t, HG_WIDTH), yb.reshape(t, FOX_WIDTH), sga, sgb,
                p[0].reshape(t, PLE_DIM),
                w_a[0].astype(BF16), w_b[0].astype(BF16), w_o[0].astype(BF16),
                row(ln1_g[0]), row(ln1_b[0]),
                w_ff1[0].astype(BF16), w_ff2[0].astype(BF16),
                w_pg[0].astype(BF16), w_p[0].astype(BF16), row(ln2_g[0]), row(ln2_b[0]),
                alpha, tm=512)
    return out.reshape(b, s, d)
```

```python
import functools

import numpy as np
import jax
import jax.numpy as jnp
from jax import lax
from jax.experimental import pallas as pl
from jax.experimental.pallas import tpu as pltpu

F32 = jnp.float32
BF16 = jnp.bfloat16

D_MODEL = 1024
HG_WIDTH = 512
HG_KDIM = 128
HG_HEADS = HG_WIDTH // HG_KDIM
FOX_WIDTH = 512
FOX_HDIM = 64
FOX_HEADS = FOX_WIDTH // FOX_HDIM
D_FF = 4 * D_MODEL
PLE_DIM = 256
LN_EPS = 1e-5
RMS_EPS = 1e-6
LANES = 128
HG_CHUNK = 64
HG_LEVELS = 6
NEG_BIG = -1e30
LOG2E = 1.4426950408889634
VMEM_LIMIT = 56 * 1024 * 1024
ROW_GROUPS = 2


def _layer_norm(x, g, b):
    mu = jnp.mean(x, axis=-1, keepdims=True)
    xc = x - mu
    var = jnp.mean(xc * xc, axis=-1, keepdims=True)
    return xc * lax.rsqrt(var + LN_EPS) * g + b


def _half_tanh(x):
    return 0.5 * jnp.tanh(0.5 * x)


def _split3(x):
    hi = x.astype(BF16)
    r = x - hi.astype(F32)
    mid = r.astype(BF16)
    lo = (r - mid.astype(F32)).astype(BF16)
    return hi, mid, lo


def _dot(a, b):
    return jnp.dot(a, b, preferred_element_type=F32)


def _dot_nt(a, b):
    return lax.dot_general(a, b, (((1,), (1,)), ((), ())), preferred_element_type=F32)


def _dot_tn(a, b):
    return lax.dot_general(a, b, (((0,), (0,)), ((), ())), preferred_element_type=F32)


def _inproj_kernel(x_ref, g0_ref, b0_ref, lb_ref, fb_ref,
                   w_hq, w_hf, w_hi, w_hg, w_fqt, w_fk, w_fvt, w_ff, w_ga, w_gb,
                   qa_ref, g_ref, kk_ref, v_ref, og_ref,
                   fqt_ref, fk_ref, fvt_ref, lf_ref, sga_ref, sgb_ref):
    lbl = lb_ref[...]
    e = jnp.exp(lbl - jnp.max(lbl, axis=0, keepdims=True))
    lb = e[0:1, :] / jnp.sum(e, axis=0, keepdims=True)

    tm = x_ref.shape[0]
    rows_per_group = tm // ROW_GROUPS
    for r in range(ROW_GROUPS):
        rows = slice(r * rows_per_group, (r + 1) * rows_per_group)
        h = _layer_norm(x_ref[rows, :], g0_ref[...], b0_ref[...]).astype(BF16)

        q = _dot(h, w_hq[...])
        qa_ref[rows, :] = (q * (0.5 + _half_tanh(q))).astype(BF16)

        tf = _half_tanh(_dot(h, w_hf[...]))
        g_ref[rows, :] = jnp.log(lb + (1.0 - lb) * (0.5 + tf))
        kk_ref[rows, :] = ((1.0 - lb) * (0.5 - tf)).astype(BF16)

        v_ref[rows, :] = _dot(h, w_hi[...]).astype(BF16)

        og = _dot(h, w_hg[...])
        og_ref[rows, :] = (og * (0.5 + _half_tanh(og))).astype(BF16)

        fqt_ref[:, rows] = (_dot_nt(w_fqt[...], h) * (FOX_HDIM ** -0.5 * LOG2E)).astype(BF16)
        fk_ref[rows, :] = _dot(h, w_fk[...]).astype(BF16)
        fvt_ref[:, rows] = _dot_nt(w_fvt[...], h).astype(BF16)

        z = _dot(h, w_ff[...]) + fb_ref[...]
        lf_ref[rows, :] = jnp.minimum(z, 0.0) - jnp.log(1.0 + jnp.exp(-jnp.abs(z)))

        sga_ref[rows, :] = (0.5 + _half_tanh(_dot(h, w_ga[...]))).astype(BF16)
        sgb_ref[rows, :] = (0.5 + _half_tanh(_dot(h, w_gb[...]))).astype(BF16)


def _const_spec(shape):
    nd = len(shape)
    return pl.BlockSpec(shape, lambda *_: (0,) * nd, pipeline_mode=pl.Buffered(1))


def _inproj(x2, ln0_g, ln0_b, hg_lb, fb_pad, ws, tm):
    t = x2.shape[0]
    tok = lambda w: pl.BlockSpec((tm, w), lambda i: (i, 0))
    out_shapes = [
        jax.ShapeDtypeStruct((t, HG_WIDTH), BF16),
        jax.ShapeDtypeStruct((t, HG_WIDTH), F32),
        jax.ShapeDtypeStruct((t, HG_WIDTH), BF16),
        jax.ShapeDtypeStruct((t, HG_WIDTH), BF16),
        jax.ShapeDtypeStruct((t, HG_WIDTH), BF16),
        jax.ShapeDtypeStruct((FOX_WIDTH, t), BF16),
        jax.ShapeDtypeStruct((t, FOX_WIDTH), BF16),
        jax.ShapeDtypeStruct((FOX_WIDTH, t), BF16),
        jax.ShapeDtypeStruct((t, LANES), F32),
        jax.ShapeDtypeStruct((t, D_MODEL), BF16),
        jax.ShapeDtypeStruct((t, D_MODEL), BF16),
    ]
    tok_t = pl.BlockSpec((FOX_WIDTH, tm), lambda i: (0, i))
    out_specs = [tok(HG_WIDTH)] * 5 + [
        tok_t, tok(FOX_WIDTH), tok_t, tok(LANES), tok(D_MODEL), tok(D_MODEL)]
    in_specs = [tok(D_MODEL), _const_spec(ln0_g.shape), _const_spec(ln0_b.shape),
                _const_spec(hg_lb.shape), _const_spec(fb_pad.shape)]
    in_specs += [_const_spec(w.shape) for w in ws]
    return pl.pallas_call(
        _inproj_kernel,
        grid=(t // tm,),
        in_specs=in_specs,
        out_specs=out_specs,
        out_shape=out_shapes,
        compiler_params=pltpu.CompilerParams(
            dimension_semantics=("parallel",), vmem_limit_bytes=VMEM_LIMIT),
        name="inproj",
    )(x2, ln0_g, ln0_b, hg_lb, fb_pad, *ws)


FOX_BIAS_BASE = (FOX_HDIM, 0)
BF16_SUBLANES = 16
FOX_VROWS = FOX_HDIM + BF16_SUBLANES


def _fox_bias_routing():
    pairs = FOX_HEADS // 2
    route = np.zeros((3 * LANES, pairs * LANES), np.float32)
    ones = np.zeros((1, pairs * LANES), np.float32)
    for head in range(FOX_HEADS):
        base = (head // 2) * LANES + FOX_BIAS_BASE[head % 2]
        for i in range(3):
            route[i * LANES + head, base + i] = 1.0
            ones[0, base + 3 + i] = 1.0
    return route, ones


def _foxcum_kernel(lf_ref, route_ref, ones_ref, kbias_ref, crow_ref, ccol_ref,
                   *, rows, route_rows):
    s = lf_ref.shape[1]
    ri = lax.broadcasted_iota(jnp.int32, (rows, rows), 0)
    ci = lax.broadcasted_iota(jnp.int32, (rows, rows), 1)
    tril = jnp.where(ci <= ri, 1.0, 0.0).astype(BF16)
    carry = jnp.zeros((1, LANES), F32)
    for i in range(s // rows):
        hi, mid, lo = _split3(lf_ref[0, i * rows:(i + 1) * rows, :])
        c = _dot(tril, hi) + _dot(tril, mid) + _dot(tril, lo) + carry
        ccol_ref[i * rows:(i + 1) * rows, :] = c * LOG2E
        carry = c[rows - 1:rows, :]
    crow_ref[0] = ccol_ref[...].T[0:FOX_HEADS, :]
    for i in range(s // route_rows):
        span = slice(i * route_rows, (i + 1) * route_rows)
        terms = jnp.concatenate(_split3(-ccol_ref[span, :]), axis=1)
        kbias_ref[0, span, :] = (_dot(terms, route_ref[...]) + ones_ref[...]).astype(BF16)


def _foxcum(lf3):
    b, s, _ = lf3.shape
    route, ones = _fox_bias_routing()
    route = jnp.asarray(route, BF16)
    ones = jnp.asarray(ones, F32)
    width = route.shape[1]
    return pl.pallas_call(
        functools.partial(_foxcum_kernel, rows=256, route_rows=min(s, 1024)),
        grid=(b,),
        in_specs=[pl.BlockSpec((1, s, LANES), lambda i: (i, 0, 0)),
                  _const_spec(route.shape), _const_spec(ones.shape)],
        out_specs=[pl.BlockSpec((1, s, width), lambda i: (i, 0, 0)),
                   pl.BlockSpec((1, FOX_HEADS, s), lambda i: (i, 0, 0))],
        out_shape=[jax.ShapeDtypeStruct((b, s, width), BF16),
                   jax.ShapeDtypeStruct((b, FOX_HEADS, s), F32)],
        scratch_shapes=[pltpu.VMEM((s, LANES), F32)],
        compiler_params=pltpu.CompilerParams(dimension_semantics=("parallel",)),
        name="foxcum",
    )(lf3, route, ones)


def _hgrn_decay_matrix():
    c = HG_CHUNK
    w = np.zeros((HG_LEVELS + 2, c, c), np.float32)
    j = np.arange(c)
    for p in range(HG_LEVELS):
        n = 1 << p
        for t in range(c):
            start = t - t % n
            if t & n:
                w[p, t] = (j >= start) & (j <= t)
            else:
                w[p, t] = (j > t) & (j <= start + n - 1)
    for t in range(c):
        w[HG_LEVELS, t] = j <= t
        w[HG_LEVELS + 1, t] = j > t
    return np.tile(w.reshape(-1, c), (1, 3))


def _hgrn_kernel(w_ref, q_ref, g_ref, k_ref, v_ref, og_ref, ng_ref, y_ref, st_ref, d_ref,
                 *, nc):
    c = HG_CHUNK

    @pl.when(pl.program_id(2) == 0)
    def _():
        st_ref[...] = jnp.zeros_like(st_ref)

    def chunks_on_lanes(x):
        return jnp.concatenate([x[i * c:(i + 1) * c, :] for i in range(nc)], axis=1)

    def lane_chunk(x, i):
        return x[:, i * LANES:(i + 1) * LANES]

    g3 = jnp.concatenate(_split3(chunks_on_lanes(g_ref[0])), axis=0)
    d_ref[...] = _dot(w_ref[...], g3)
    q = chunks_on_lanes(q_ref[0])
    k = chunks_on_lanes(k_ref[0])
    v = v_ref[0]

    def decay(block):
        return jnp.exp(d_ref[block * c:(block + 1) * c, :])

    ti = lax.broadcasted_iota(jnp.int32, (c, c), 0)
    si = lax.broadcasted_iota(jnp.int32, (c, c), 1)
    x = jnp.bitwise_xor(ti, si)
    level = jnp.zeros((c, c), jnp.int32)
    for p in range(HG_LEVELS):
        level = level + jnp.where(x >= (1 << p), 1, 0)
    level = jnp.where(ti >= si, level, -1)

    scores = [jnp.where(level == 0, _dot_nt(lane_chunk(q, i), lane_chunk(k, i)), 0.0)
              for i in range(nc)]
    for p in range(HG_LEVELS):
        e = decay(p).astype(BF16)
        qt = q * e
        kt = k * e
        for i in range(nc):
            a = _dot_nt(lane_chunk(qt, i), lane_chunk(kt, i))
            scores[i] = jnp.where(level == p + 1, a, scores[i])

    e_start = decay(HG_LEVELS)
    q_in = q * e_start.astype(BF16)
    k_out = k * decay(HG_LEVELS + 1).astype(BF16)
    ng = ng_ref[...]

    v_chunks = [v[i * c:(i + 1) * c, :] for i in range(nc)]
    updates = [_dot_tn(v_chunks[i], lane_chunk(k_out, i)) for i in range(nc)]
    st = st_ref[...]
    states = []
    for i in range(nc):
        states.append(st.astype(BF16))
        st = st * lane_chunk(e_start, i)[c - 1:c, :] + updates[i]
    st_ref[...] = st
    for i in range(nc):
        o = (_dot(scores[i].astype(BF16), v_chunks[i])
             + _dot_nt(lane_chunk(q_in, i), states[i]))
        o = o * lax.rsqrt(jnp.mean(o * o, axis=-1, keepdims=True) + RMS_EPS)
        gate = og_ref[0, i * c:(i + 1) * c, :].astype(F32)
        y_ref[0, i * c:(i + 1) * c, :] = (o * ng * gate).astype(BF16)


def _hgrn(qa, g, kk, v, og, norm_g, tc):
    b, s, _ = qa.shape
    nc = tc // HG_CHUNK
    w = jnp.asarray(_hgrn_decay_matrix(), BF16)
    tok = pl.BlockSpec((1, tc, HG_KDIM), lambda bi, hi, si: (bi, si, hi))
    return pl.pallas_call(
        functools.partial(_hgrn_kernel, nc=nc),
        grid=(b, HG_HEADS, s // tc),
        in_specs=[_const_spec(w.shape), tok, tok, tok, tok, tok,
                  pl.BlockSpec((1, HG_KDIM), lambda bi, hi, si: (0, hi))],
        out_specs=tok,
        out_shape=jax.ShapeDtypeStruct((b, s, HG_WIDTH), BF16),
        scratch_shapes=[pltpu.VMEM((HG_KDIM, HG_KDIM), F32),
                        pltpu.VMEM(((HG_LEVELS + 2) * HG_CHUNK, nc * HG_KDIM), F32)],
        compiler_params=pltpu.CompilerParams(
            dimension_semantics=("parallel", "parallel", "arbitrary"),
            vmem_limit_bytes=VMEM_LIMIT),
        name="hgrn",
    )(w, qa, g, kk, v, og, norm_g)


def _fox_kernel(qt_ref, k_ref, vt_ref, kbias_ref, crow_ref, o_ref,
                kaug_ref, vaug_ref, qaug_ref, m_ref, acc_ref, s_ref, mx_ref, *, tq):
    s_len = k_ref.shape[1]
    nq = s_len // tq
    half = tq // 2
    aug_base = FOX_BIAS_BASE
    def build_keys_and_values():
        lane = lax.broadcasted_iota(jnp.int32, (s_len, LANES), 1)
        pad_row = lax.broadcasted_iota(jnp.int32, (FOX_VROWS - FOX_HDIM, s_len), 0)
        ones_rows = jnp.where(pad_row == 0, 1.0, 0.0).astype(BF16)
        k = k_ref[0]
        kbias = kbias_ref[0]
        for hh in range(2):
            is_own = (lane < FOX_HDIM) if hh == 0 else (lane >= FOX_HDIM)
            kaug_ref[hh] = jnp.where(is_own, k, kbias)
            vaug_ref[hh, 0:FOX_HDIM, :] = vt_ref[hh * FOX_HDIM:(hh + 1) * FOX_HDIM, :]
            vaug_ref[hh, FOX_HDIM:FOX_VROWS, :] = ones_rows

    row = lax.broadcasted_iota(jnp.int32, (LANES, tq), 0)

    def build_queries(qi):
        qt = qt_ref[:, qi * tq:(qi + 1) * tq].astype(F32)
        for hh in range(2):
            hi, mid, lo = _split3(crow_ref[0, 0, hh:hh + 1, qi * tq:qi * tq + LANES][:, 0:1])
            rel = row - aug_base[hh]
            aug = jnp.where(rel == 3, hi.astype(F32),
                            jnp.where(rel == 4, mid.astype(F32),
                                      jnp.where(rel == 5, lo.astype(F32),
                                                jnp.where(rel < 3, 1.0, 0.0))))
            aug = jnp.where(rel >= 0, aug, 0.0)
            is_own = (row < FOX_HDIM) if hh == 0 else (row >= FOX_HDIM)
            qaug_ref[qi, hh] = jnp.where(is_own, qt, aug).astype(BF16)

    build_keys_and_values()
    for qi in range(nq):
        build_queries(qi)
    m_ref[...] = jnp.full(m_ref.shape, NEG_BIG, F32)
    acc_ref[...] = jnp.zeros(acc_ref.shape, F32)

    def issue_scores(nxt, slot, part, mx):
        qi_next, k_next, diagonal = nxt
        rows = slice(part * half, (part + 1) * half)
        for hh in range(2):
            keys = kaug_ref[hh, pl.ds(k_next + part * half, half), :]
            if diagonal and part == 1:
                st = _dot(keys, qaug_ref[qi_next, hh, :, half:])
                ki = lax.broadcasted_iota(jnp.int32, (half, half), 0)
                ti = lax.broadcasted_iota(jnp.int32, (half, half), 1)
                st = jnp.concatenate([jnp.full((half, half), NEG_BIG, F32),
                                      jnp.where(ki <= ti, st, NEG_BIG)], axis=1)
            else:
                st = _dot(keys, qaug_ref[qi_next, hh])
                if diagonal:
                    ki = lax.broadcasted_iota(jnp.int32, (half, tq), 0)
                    ti = lax.broadcasted_iota(jnp.int32, (half, tq), 1)
                    st = jnp.where(ki <= ti, st, NEG_BIG)
            s_ref[slot, hh, rows, :] = st
            part_max = jnp.max(st, axis=0, keepdims=True)
            mx[hh] = part_max if part == 0 else jnp.maximum(mx[hh], part_max)

    def step(qi_cur, k_cur, cur, nxt):
        m_new, scales = [], []
        for hh in range(2):
            m_old = m_ref[qi_cur, hh]
            m_new.append(jnp.maximum(m_old, mx_ref[cur, hh]))
            scales.append(jnp.exp2(m_old - m_new[hh]))
            m_ref[qi_cur, hh] = m_new[hh]
        pv = [None, None]
        mx = [None, None]
        for part in range(2):
            rows = slice(part * half, (part + 1) * half)
            if nxt is not None:
                issue_scores(nxt, 1 - cur, part, mx)
            for hh in range(2):
                p = jnp.exp2(s_ref[cur, hh, rows, :] - m_new[hh]).astype(BF16)
                d = _dot(vaug_ref[hh, :, pl.ds(k_cur + part * half, half)], p)
                pv[hh] = d if part == 0 else pv[hh] + d
        for hh in range(2):
            acc_ref[qi_cur, hh] = scales[hh] * acc_ref[qi_cur, hh] + pv[hh]
            if nxt is not None:
                mx_ref[1 - cur, hh] = mx[hh]

    def finish(qi):
        heads = []
        for hh in range(2):
            acc = acc_ref[qi, hh]
            heads.append(acc[0:FOX_HDIM, :] / acc[FOX_HDIM:FOX_HDIM + 1, :])
        ot = jnp.concatenate(heads, axis=0)
        o_ref[0, qi * tq:(qi + 1) * tq, :] = ot.T.astype(BF16)

    def diagonal_of(qi):
        return (qi, qi * tq, True) if qi < nq else None

    mx0 = [None, None]
    for part in range(2):
        issue_scores(diagonal_of(0), 0, part, mx0)
    for hh in range(2):
        mx_ref[0, hh] = mx0[hh]

    slot = 0
    for qi in range(nq):
        after = diagonal_of(qi + 1)
        step(qi, qi * tq, slot, (qi, 0, False) if qi > 0 else after)
        slot = 1 - slot
        if qi > 0:
            n_inner = qi - 1
            if n_inner // 2 > 0:
                first = slot

                def two_steps(jj, carry, qi=qi, first=first):
                    k0 = pl.multiple_of(2 * jj * tq, tq)
                    step(qi, k0, first, (qi, k0 + tq, False))
                    step(qi, k0 + tq, 1 - first, (qi, k0 + 2 * tq, False))
                    return carry

                lax.fori_loop(0, n_inner // 2, two_steps, 0)
            if n_inner % 2 == 1:
                step(qi, (n_inner - 1) * tq, slot, (qi, n_inner * tq, False))
                slot = 1 - slot
            step(qi, n_inner * tq, slot, after)
            slot = 1 - slot
        finish(qi)


def _fox(fqt, fk, fvt, kbias, crow4, tq):
    b, s, _ = fk.shape
    pairs = FOX_HEADS // 2
    nq = s // tq
    return pl.pallas_call(
        functools.partial(_fox_kernel, tq=tq),
        grid=(b, pairs),
        in_specs=[
            pl.BlockSpec((LANES, s), lambda bi, hp: (hp, bi)),
            pl.BlockSpec((1, s, LANES), lambda bi, hp: (bi, 0, hp)),
            pl.BlockSpec((LANES, s), lambda bi, hp: (hp, bi)),
            pl.BlockSpec((1, s, LANES), lambda bi, hp: (bi, 0, hp)),
            pl.BlockSpec((1, 1, 2, s), lambda bi, hp: (bi, hp, 0, 0)),
        ],
        out_specs=pl.BlockSpec((1, s, LANES), lambda bi, hp: (bi, 0, hp)),
        out_shape=jax.ShapeDtypeStruct((b, s, FOX_WIDTH), BF16),
        scratch_shapes=[pltpu.VMEM((2, s, LANES), BF16), pltpu.VMEM((2, FOX_VROWS, s), BF16),
                        pltpu.VMEM((nq, 2, LANES, tq), BF16),
                        pltpu.VMEM((nq, 2, 1, tq), F32), pltpu.VMEM((nq, 2, FOX_VROWS, tq), F32),
                        pltpu.VMEM((2, 2, tq, tq), F32), pltpu.VMEM((2, 2, 1, tq), F32)],
        compiler_params=pltpu.CompilerParams(
            dimension_semantics=("parallel", "parallel"),
            vmem_limit_bytes=VMEM_LIMIT),
        name="fox",
    )(fqt, fk, fvt, kbias, crow4)


def _tail_kernel(x_ref, g0_ref, b0_ref, ya_ref, yb_ref, sga_ref, sgb_ref, p_ref,
                 wa_ref, wb_ref, wo_ref, g1_ref, b1_ref,
                 w1_ref, w2_ref, wpg_ref, wp_ref, g2_ref, b2_ref, o_ref, *, alpha, fc):
    rows_per_group = x_ref.shape[0] // ROW_GROUPS
    groups = [slice(r * rows_per_group, (r + 1) * rows_per_group) for r in range(ROW_GROUPS)]

    def mixer(rows):
        h0 = _layer_norm(x_ref[rows, :], g0_ref[...], b0_ref[...])
        merged = (sga_ref[rows, :].astype(F32) * _dot(ya_ref[rows, :], wa_ref[...])
                  + sgb_ref[rows, :].astype(F32) * _dot(yb_ref[rows, :], wb_ref[...]))
        mix = _dot(merged.astype(BF16), wo_ref[...])
        return _layer_norm(alpha * h0 + mix, g1_ref[...], b1_ref[...])

    def channel_mixer(rows, h):
        hb = h.astype(BF16)
        sg = 0.5 + _half_tanh(_dot(hb, wpg_ref[...]))
        acc = alpha * h + sg * _dot(p_ref[rows, :].astype(BF16), wp_ref[...])
        for f in range(0, D_FF, fc):
            u = jnp.maximum(_dot(hb, w1_ref[:, f:f + fc]), 0.0)
            acc = acc + _dot((u * u).astype(BF16), w2_ref[f:f + fc, :])
        o_ref[rows, :] = _layer_norm(acc, g2_ref[...], b2_ref[...])

    h1 = [mixer(rows) for rows in groups]
    for rows, h in zip(groups, h1):
        channel_mixer(rows, h)


def _tail(x2, ln0_g, ln0_b, ya, yb, sga, sgb, p2, wa, wb, wo, ln1_g, ln1_b,
          w1, w2, wpg, wp, ln2_g, ln2_b, alpha, tm):
    t = x2.shape[0]
    tok = lambda w: pl.BlockSpec((tm, w), lambda i: (i, 0))
    consts = [wa, wb, wo, ln1_g, ln1_b, w1, w2, wpg, wp, ln2_g, ln2_b]
    return pl.pallas_call(
        functools.partial(_tail_kernel, alpha=alpha, fc=1024),
        grid=(t // tm,),
        in_specs=[tok(D_MODEL), _const_spec(ln0_g.shape), _const_spec(ln0_b.shape),
                  tok(HG_WIDTH), tok(FOX_WIDTH), tok(D_MODEL), tok(D_MODEL), tok(PLE_DIM)]
        + [_const_spec(a.shape) for a in consts],
        out_specs=tok(D_MODEL),
        out_shape=jax.ShapeDtypeStruct((t, D_MODEL), F32),
        compiler_params=pltpu.CompilerParams(
            dimension_semantics=("parallel",), vmem_limit_bytes=VMEM_LIMIT),
        name="tail",
    )(x2, ln0_g, ln0_b, ya, yb, sga, sgb, p2, wa, wb, wo, ln1_g, ln1_b,
      w1, w2, wpg, wp, ln2_g, ln2_b)


def kernel(x, p, ln0_g, ln0_b, w_in, hg_lb, hg_norm_g, fox_fb, w_a, w_b, w_o,
           ln1_g, ln1_b, w_ff1, w_ff2, w_pg, w_p, ln2_g, ln2_b):
    b, s, d = x.shape
    depth = w_in.shape[0]
    assert depth == 1 and d == D_MODEL and hg_lb.shape[0] == 2
    assert s % 512 == 0
    t = b * s
    alpha = (2.0 * depth) ** 0.25
    row = lambda a: a.reshape(1, -1).astype(F32)

    sizes = [HG_WIDTH] * 4 + [FOX_WIDTH] * 3 + [FOX_HEADS] + [D_MODEL] * 2
    offs = np.concatenate([[0], np.cumsum(sizes)])
    wi = w_in[0]
    parts = [wi[:, offs[i]:offs[i + 1]] for i in range(len(sizes))]
    w_hq, w_hf, w_hi, w_hg, w_fq, w_fk, w_fv, w_ff, w_ga, w_gb = parts
    w_ff = jnp.pad(w_ff, ((0, 0), (0, LANES - FOX_HEADS)))
    fb_pad = jnp.pad(fox_fb[0].astype(F32), (0, LANES - FOX_HEADS)).reshape(1, LANES)
    ws = [w.astype(BF16) for w in
          (w_hq, w_hf, w_hi, w_hg, w_fq.T, w_fk, w_fv.T, w_ff, w_ga, w_gb)]

    x2 = x.reshape(t, d)
    g0, b0 = row(ln0_g), row(ln0_b)
    (qa, g, kk, v, og, fqt, fk, fvt, lf, sga, sgb) = _inproj(
        x2, g0, b0, hg_lb.astype(F32), fb_pad, ws, tm=1024)

    kbias, crow = _foxcum(lf.reshape(b, s, LANES))
    crow4 = crow.reshape(b, FOX_HEADS // 2, 2, s)

    r3 = lambda a: a.reshape(b, s, a.shape[-1])
    ya = _hgrn(r3(qa), r3(g), r3(kk), r3(v), r3(og), hg_norm_g.astype(F32), tc=512)
    yb = _fox(fqt, r3(fk), fvt, kbias, crow4, tq=512)

    out = _tail(x2, g0, b0, ya.reshape(t, HG_WIDTH), yb.reshape(t, FOX_WIDTH), sga, sgb,
                p[0].reshape(t, PLE_DIM),
                w_a[0].astype(BF16), w_b[0].astype(BF16), w_o[0].astype(BF16),
                row(ln1_g[0]), row(ln1_b[0]),
                w_ff1[0].astype(BF16), w_ff2[0].astype(BF16),
                w_pg[0].astype(BF16), w_p[0].astype(BF16), row(ln2_g[0]), row(ln2_b[0]),
                alpha, tm=512)
    return out.reshape(b, s, d)
```

```python
import functools

import numpy as np
import jax
import jax.numpy as jnp
from jax import lax
from jax.experimental import pallas as pl
from jax.experimental.pallas import tpu as pltpu

F32 = jnp.float32
BF16 = jnp.bfloat16

D_MODEL = 1024
HG_WIDTH = 512
HG_KDIM = 128
HG_HEADS = HG_WIDTH // HG_KDIM
FOX_WIDTH = 512
FOX_HDIM = 64
FOX_HEADS = FOX_WIDTH // FOX_HDIM
D_FF = 4 * D_MODEL
PLE_DIM = 256
LN_EPS = 1e-5
RMS_EPS = 1e-6
LANES = 128
HG_CHUNK = 64
HG_LEVELS = 6
HG_HEADS_PER_STEP = 4
NEG_BIG = -1e30
LOG2E = 1.4426950408889634
VMEM_LIMIT = 56 * 1024 * 1024
ROW_GROUPS = 2


def _layer_norm(x, g, b):
    mu = jnp.mean(x, axis=-1, keepdims=True)
    xc = x - mu
    var = jnp.mean(xc * xc, axis=-1, keepdims=True)
    return xc * lax.rsqrt(var + LN_EPS) * g + b


def _half_tanh(x):
    return 0.5 * jnp.tanh(0.5 * x)


def _split3(x):
    hi = x.astype(BF16)
    r = x - hi.astype(F32)
    mid = r.astype(BF16)
    lo = (r - mid.astype(F32)).astype(BF16)
    return hi, mid, lo


def _dot(a, b):
    return jnp.dot(a, b, preferred_element_type=F32)


def _dot_nt(a, b):
    return lax.dot_general(a, b, (((1,), (1,)), ((), ())), preferred_element_type=F32)


def _dot_tn(a, b):
    return lax.dot_general(a, b, (((0,), (0,)), ((), ())), preferred_element_type=F32)


def _inproj_kernel(x_ref, g0_ref, b0_ref, lb_ref, fb_ref,
                   w_hq, w_hf, w_hi, w_hg, w_fqt, w_fk, w_fvt, w_ff, w_ga, w_gb,
                   qa_ref, g_ref, kk_ref, v_ref, og_ref,
                   fqt_ref, fk_ref, fvt_ref, lf_ref, sga_ref, sgb_ref):
    lbl = lb_ref[...]
    e = jnp.exp(lbl - jnp.max(lbl, axis=0, keepdims=True))
    lb = e[0:1, :] / jnp.sum(e, axis=0, keepdims=True)

    tm = x_ref.shape[0]
    rows_per_group = tm // ROW_GROUPS
    for r in range(ROW_GROUPS):
        rows = slice(r * rows_per_group, (r + 1) * rows_per_group)
        h = _layer_norm(x_ref[rows, :], g0_ref[...], b0_ref[...]).astype(BF16)

        q = _dot(h, w_hq[...])
        qa_ref[rows, :] = (q * (0.5 + _half_tanh(q))).astype(BF16)

        tf = _half_tanh(_dot(h, w_hf[...]))
        g_ref[rows, :] = jnp.log(lb + (1.0 - lb) * (0.5 + tf))
        kk_ref[rows, :] = ((1.0 - lb) * (0.5 - tf)).astype(BF16)

        v_ref[rows, :] = _dot(h, w_hi[...]).astype(BF16)

        og = _dot(h, w_hg[...])
        og_ref[rows, :] = (og * (0.5 + _half_tanh(og))).astype(BF16)

        fqt_ref[:, rows] = (_dot_nt(w_fqt[...], h) * (FOX_HDIM ** -0.5 * LOG2E)).astype(BF16)
        fk_ref[rows, :] = _dot(h, w_fk[...]).astype(BF16)
        fvt_ref[:, rows] = _dot_nt(w_fvt[...], h).astype(BF16)

        z = _dot(h, w_ff[...]) + fb_ref[...]
        lf_ref[rows, :] = jnp.minimum(z, 0.0) - jnp.log(1.0 + jnp.exp(-jnp.abs(z)))

        sga_ref[rows, :] = (0.5 + _half_tanh(_dot(h, w_ga[...]))).astype(BF16)
        sgb_ref[rows, :] = (0.5 + _half_tanh(_dot(h, w_gb[...]))).astype(BF16)


def _const_spec(shape):
    nd = len(shape)
    return pl.BlockSpec(shape, lambda *_: (0,) * nd, pipeline_mode=pl.Buffered(1))


def _inproj(x2, ln0_g, ln0_b, hg_lb, fb_pad, ws, tm):
    t = x2.shape[0]
    tok = lambda w: pl.BlockSpec((tm, w), lambda i: (i, 0))
    out_shapes = [
        jax.ShapeDtypeStruct((t, HG_WIDTH), BF16),
        jax.ShapeDtypeStruct((t, HG_WIDTH), F32),
        jax.ShapeDtypeStruct((t, HG_WIDTH), BF16),
        jax.ShapeDtypeStruct((t, HG_WIDTH), BF16),
        jax.ShapeDtypeStruct((t, HG_WIDTH), BF16),
        jax.ShapeDtypeStruct((FOX_WIDTH, t), BF16),
        jax.ShapeDtypeStruct((t, FOX_WIDTH), BF16),
        jax.ShapeDtypeStruct((FOX_WIDTH, t), BF16),
        jax.ShapeDtypeStruct((t, LANES), F32),
        jax.ShapeDtypeStruct((t, D_MODEL), BF16),
        jax.ShapeDtypeStruct((t, D_MODEL), BF16),
    ]
    tok_t = pl.BlockSpec((FOX_WIDTH, tm), lambda i: (0, i))
    out_specs = [tok(HG_WIDTH)] * 5 + [
        tok_t, tok(FOX_WIDTH), tok_t, tok(LANES), tok(D_MODEL), tok(D_MODEL)]
    in_specs = [tok(D_MODEL), _const_spec(ln0_g.shape), _const_spec(ln0_b.shape),
                _const_spec(hg_lb.shape), _const_spec(fb_pad.shape)]
    in_specs += [_const_spec(w.shape) for w in ws]
    return pl.pallas_call(
        _inproj_kernel,
        grid=(t // tm,),
        in_specs=in_specs,
        out_specs=out_specs,
        out_shape=out_shapes,
        compiler_params=pltpu.CompilerParams(
            dimension_semantics=("parallel",), vmem_limit_bytes=VMEM_LIMIT),
        name="inproj",
    )(x2, ln0_g, ln0_b, hg_lb, fb_pad, *ws)


FOX_BIAS_BASE = (FOX_HDIM, 0)
BF16_SUBLANES = 16
FOX_VROWS = FOX_HDIM + BF16_SUBLANES


def _fox_bias_routing():
    pairs = FOX_HEADS // 2
    route = np.zeros((3 * LANES, pairs * LANES), np.float32)
    ones = np.zeros((1, pairs * LANES), np.float32)
    for head in range(FOX_HEADS):
        base = (head // 2) * LANES + FOX_BIAS_BASE[head % 2]
        for i in range(3):
            route[i * LANES + head, base + i] = 1.0
            ones[0, base + 3 + i] = 1.0
    return route, ones


def _foxcum_kernel(lf_ref, route_ref, ones_ref, kbias_ref, crow_ref, ccol_ref,
                   *, rows, route_rows):
    s = lf_ref.shape[1]
    ri = lax.broadcasted_iota(jnp.int32, (rows, rows), 0)
    ci = lax.broadcasted_iota(jnp.int32, (rows, rows), 1)
    tril = jnp.where(ci <= ri, 1.0, 0.0).astype(BF16)
    carry = jnp.zeros((1, LANES), F32)
    for i in range(s // rows):
        hi, mid, lo = _split3(lf_ref[0, i * rows:(i + 1) * rows, :])
        c = _dot(tril, hi) + _dot(tril, mid) + _dot(tril, lo) + carry
        ccol_ref[i * rows:(i + 1) * rows, :] = c * LOG2E
        carry = c[rows - 1:rows, :]
    crow_ref[0] = ccol_ref[...].T[0:FOX_HEADS, :]
    for i in range(s // route_rows):
        span = slice(i * route_rows, (i + 1) * route_rows)
        terms = jnp.concatenate(_split3(-ccol_ref[span, :]), axis=1)
        kbias_ref[0, span, :] = (_dot(terms, route_ref[...]) + ones_ref[...]).astype(BF16)


def _foxcum(lf3):
    b, s, _ = lf3.shape
    route, ones = _fox_bias_routing()
    route = jnp.asarray(route, BF16)
    ones = jnp.asarray(ones, F32)
    width = route.shape[1]
    return pl.pallas_call(
        functools.partial(_foxcum_kernel, rows=256, route_rows=min(s, 1024)),
        grid=(b,),
        in_specs=[pl.BlockSpec((1, s, LANES), lambda i: (i, 0, 0)),
                  _const_spec(route.shape), _const_spec(ones.shape)],
        out_specs=[pl.BlockSpec((1, s, width), lambda i: (i, 0, 0)),
                   pl.BlockSpec((1, FOX_HEADS, s), lambda i: (i, 0, 0))],
        out_shape=[jax.ShapeDtypeStruct((b, s, width), BF16),
                   jax.ShapeDtypeStruct((b, FOX_HEADS, s), F32)],
        scratch_shapes=[pltpu.VMEM((s, LANES), F32)],
        compiler_params=pltpu.CompilerParams(dimension_semantics=("parallel",)),
        name="foxcum",
    )(lf3, route, ones)


def _hgrn_decay_matrix():
    c = HG_CHUNK
    w = np.zeros((HG_LEVELS + 2, c, c), np.float32)
    j = np.arange(c)
    for p in range(HG_LEVELS):
        n = 1 << p
        for t in range(c):
            start = t - t % n
            if t & n:
                w[p, t] = (j >= start) & (j <= t)
            else:
                w[p, t] = (j > t) & (j <= start + n - 1)
    for t in range(c):
        w[HG_LEVELS, t] = j <= t
        w[HG_LEVELS + 1, t] = j > t
    return np.tile(w.reshape(-1, c), (1, 3))


def _hgrn_kernel(w_ref, q_ref, g_ref, k_ref, v_ref, og_ref, ng_ref, y_ref, st_ref, d_ref,
                 *, nc):
    c = HG_CHUNK

    @pl.when(pl.program_id(2) == 0)
    def _():
        st_ref[...] = jnp.zeros_like(st_ref)

    def chunks_on_lanes(x):
        return jnp.concatenate([x[i * c:(i + 1) * c, :] for i in range(nc)], axis=1)

    def lane_chunk(x, i):
        return x[:, i * LANES:(i + 1) * LANES]

    ti = lax.broadcasted_iota(jnp.int32, (c, c), 0)
    si = lax.broadcasted_iota(jnp.int32, (c, c), 1)
    x = jnp.bitwise_xor(ti, si)
    level = jnp.zeros((c, c), jnp.int32)
    for p in range(HG_LEVELS):
        level = level + jnp.where(x >= (1 << p), 1, 0)
    level = jnp.where(ti >= si, level, -1)

    heads = range(HG_HEADS_PER_STEP)
    lanes = [slice(h * HG_KDIM, (h + 1) * HG_KDIM) for h in heads]

    def decay(h, block):
        return jnp.exp(d_ref[h, block * c:(block + 1) * c, :])

    for h in heads:
        g3 = jnp.concatenate(_split3(chunks_on_lanes(g_ref[0, :, lanes[h]])), axis=0)
        d_ref[h] = _dot(w_ref[...], g3)

    def within_chunks(h):
        q = chunks_on_lanes(q_ref[0, :, lanes[h]])
        k = chunks_on_lanes(k_ref[0, :, lanes[h]])
        q0 = q * decay(h, 0).astype(BF16)
        scores = []
        for i in range(nc):
            both = _dot_nt(jnp.concatenate([lane_chunk(q, i), lane_chunk(q0, i)], axis=0),
                           lane_chunk(k, i))
            scores.append(jnp.where(level == 0, both[0:c, :],
                                    jnp.where(level == 1, both[c:2 * c, :], 0.0)))
        for p in range(1, HG_LEVELS):
            e = decay(h, p).astype(BF16)
            qt = q * e
            kt = k * e
            for i in range(nc):
                a = _dot_nt(lane_chunk(qt, i), lane_chunk(kt, i))
                scores[i] = jnp.where(level == p + 1, a, scores[i])
        return q, k, scores

    def state_chain(h, k):
        e_start = decay(h, HG_LEVELS)
        k_out = k * decay(h, HG_LEVELS + 1).astype(BF16)
        v = v_ref[0, :, lanes[h]]
        v_chunks = [v[i * c:(i + 1) * c, :] for i in range(nc)]
        updates = [_dot_tn(v_chunks[i], lane_chunk(k_out, i)) for i in range(nc)]
        st = st_ref[h]
        states = []
        for i in range(nc):
            states.append(st.astype(BF16))
            st = st * lane_chunk(e_start, i)[c - 1:c, :] + updates[i]
        st_ref[h] = st
        return e_start, v_chunks, states

    def outputs(h, q, scores, e_start, v_chunks, states):
        q_in = q * e_start.astype(BF16)
        ng = ng_ref[:, lanes[h]]
        for i in range(nc):
            o = (_dot(scores[i].astype(BF16), v_chunks[i])
                 + _dot_nt(lane_chunk(q_in, i), states[i]))
            o = o * lax.rsqrt(jnp.mean(o * o, axis=-1, keepdims=True) + RMS_EPS)
            gate = og_ref[0, i * c:(i + 1) * c, lanes[h]].astype(F32)
            y_ref[0, i * c:(i + 1) * c, lanes[h]] = (o * ng * gate).astype(BF16)

    within = {}
    chain = {}
    for h in heads:
        within[h] = within_chunks(h)
        chain[h] = state_chain(h, within[h][1])
    for h in heads:
        q, _, scores = within[h]
        outputs(h, q, scores, *chain[h])


def _hgrn(qa, g, kk, v, og, norm_g, tc):
    b, s, _ = qa.shape
    nc = tc // HG_CHUNK
    width = HG_HEADS_PER_STEP * HG_KDIM
    w = jnp.asarray(_hgrn_decay_matrix(), BF16)
    tok = pl.BlockSpec((1, tc, width), lambda bi, hi, si: (bi, si, hi))
    return pl.pallas_call(
        functools.partial(_hgrn_kernel, nc=nc),
        grid=(b, HG_HEADS // HG_HEADS_PER_STEP, s // tc),
        in_specs=[_const_spec(w.shape), tok, tok, tok, tok, tok,
                  pl.BlockSpec((1, width), lambda bi, hi, si: (0, hi))],
        out_specs=tok,
        out_shape=jax.ShapeDtypeStruct((b, s, HG_WIDTH), BF16),
        scratch_shapes=[pltpu.VMEM((HG_HEADS_PER_STEP, HG_KDIM, HG_KDIM), F32),
                        pltpu.VMEM((HG_HEADS_PER_STEP, (HG_LEVELS + 2) * HG_CHUNK,
                                    nc * HG_KDIM), F32)],
        compiler_params=pltpu.CompilerParams(
            dimension_semantics=("parallel", "parallel", "arbitrary"),
            vmem_limit_bytes=VMEM_LIMIT),
        name="hgrn",
    )(w, qa, g, kk, v, og, norm_g)


def _fox_kernel(qt_ref, k_ref, vt_ref, kbias_ref, crow_ref, o_ref,
                kaug_ref, vaug_ref, qaug_ref, m_ref, acc_ref, s_ref, mx_ref, *, tq):
    s_len = k_ref.shape[1]
    nq = s_len // tq
    half = tq // 2
    aug_base = FOX_BIAS_BASE
    def build_keys_and_values():
        lane = lax.broadcasted_iota(jnp.int32, (s_len, LANES), 1)
        pad_row = lax.broadcasted_iota(jnp.int32, (FOX_VROWS - FOX_HDIM, s_len), 0)
        ones_rows = jnp.where(pad_row == 0, 1.0, 0.0).astype(BF16)
        k = k_ref[0]
        kbias = kbias_ref[0]
        for hh in range(2):
            is_own = (lane < FOX_HDIM) if hh == 0 else (lane >= FOX_HDIM)
            kaug_ref[hh] = jnp.where(is_own, k, kbias)
            vaug_ref[hh, 0:FOX_HDIM, :] = vt_ref[hh * FOX_HDIM:(hh + 1) * FOX_HDIM, :]
            vaug_ref[hh, FOX_HDIM:FOX_VROWS, :] = ones_rows

    row = lax.broadcasted_iota(jnp.int32, (LANES, tq), 0)

    def build_queries(qi):
        qt = qt_ref[:, qi * tq:(qi + 1) * tq].astype(F32)
        for hh in range(2):
            hi, mid, lo = _split3(crow_ref[0, 0, hh:hh + 1, qi * tq:qi * tq + LANES][:, 0:1])
            rel = row - aug_base[hh]
            aug = jnp.where(rel == 3, hi.astype(F32),
                            jnp.where(rel == 4, mid.astype(F32),
                                      jnp.where(rel == 5, lo.astype(F32),
                                                jnp.where(rel < 3, 1.0, 0.0))))
            aug = jnp.where(rel >= 0, aug, 0.0)
            is_own = (row < FOX_HDIM) if hh == 0 else (row >= FOX_HDIM)
            qaug_ref[qi, hh] = jnp.where(is_own, qt, aug).astype(BF16)

    build_keys_and_values()
    for qi in range(nq):
        build_queries(qi)
    m_ref[...] = jnp.full(m_ref.shape, NEG_BIG, F32)
    acc_ref[...] = jnp.zeros(acc_ref.shape, F32)

    def issue_scores(nxt, slot, part, mx):
        qi_next, k_next, diagonal = nxt
        rows = slice(part * half, (part + 1) * half)
        for hh in range(2):
            keys = kaug_ref[hh, pl.ds(k_next + part * half, half), :]
            if diagonal and part == 1:
                st = _dot(keys, qaug_ref[qi_next, hh, :, half:])
                ki = lax.broadcasted_iota(jnp.int32, (half, half), 0)
                ti = lax.broadcasted_iota(jnp.int32, (half, half), 1)
                st = jnp.concatenate([jnp.full((half, half), NEG_BIG, F32),
                                      jnp.where(ki <= ti, st, NEG_BIG)], axis=1)
            else:
                st = _dot(keys, qaug_ref[qi_next, hh])
                if diagonal:
                    ki = lax.broadcasted_iota(jnp.int32, (half, tq), 0)
                    ti = lax.broadcasted_iota(jnp.int32, (half, tq), 1)
                    st = jnp.where(ki <= ti, st, NEG_BIG)
            s_ref[slot, hh, rows, :] = st
            part_max = jnp.max(st, axis=0, keepdims=True)
            mx[hh] = part_max if part == 0 else jnp.maximum(mx[hh], part_max)

    def step(qi_cur, k_cur, cur, nxt):
        m_new, scales = [], []
        for hh in range(2):
            m_old = m_ref[qi_cur, hh]
            m_new.append(jnp.maximum(m_old, mx_ref[cur, hh]))
            scales.append(jnp.exp2(m_old - m_new[hh]))
            m_ref[qi_cur, hh] = m_new[hh]
        pv = [None, None]
        mx = [None, None]
        for part in range(2):
            rows = slice(part * half, (part + 1) * half)
            if nxt is not None:
                issue_scores(nxt, 1 - cur, part, mx)
            for hh in range(2):
                p = jnp.exp2(s_ref[cur, hh, rows, :] - m_new[hh]).astype(BF16)
                d = _dot(vaug_ref[hh, :, pl.ds(k_cur + part * half, half)], p)
                pv[hh] = d if part == 0 else pv[hh] + d
        for hh in range(2):
            acc_ref[qi_cur, hh] = scales[hh] * acc_ref[qi_cur, hh] + pv[hh]
            if nxt is not None:
                mx_ref[1 - cur, hh] = mx[hh]

    def finish(qi):
        heads = []
        for hh in range(2):
            acc = acc_ref[qi, hh]
            heads.append(acc[0:FOX_HDIM, :] / acc[FOX_HDIM:FOX_HDIM + 1, :])
        ot = jnp.concatenate(heads, axis=0)
        o_ref[0, qi * tq:(qi + 1) * tq, :] = ot.T.astype(BF16)

    def diagonal_of(qi):
        return (qi, qi * tq, True) if qi < nq else None

    mx0 = [None, None]
    for part in range(2):
        issue_scores(diagonal_of(0), 0, part, mx0)
    for hh in range(2):
        mx_ref[0, hh] = mx0[hh]

    slot = 0
    for qi in range(nq):
        after = diagonal_of(qi + 1)
        step(qi, qi * tq, slot, (qi, 0, False) if qi > 0 else after)
        slot = 1 - slot
        if qi > 0:
            n_inner = qi - 1
            if n_inner // 2 > 0:
                first = slot

                def two_steps(jj, carry, qi=qi, first=first):
                    k0 = pl.multiple_of(2 * jj * tq, tq)
                    step(qi, k0, first, (qi, k0 + tq, False))
                    step(qi, k0 + tq, 1 - first, (qi, k0 + 2 * tq, False))
                    return carry

                lax.fori_loop(0, n_inner // 2, two_steps, 0)
            if n_inner % 2 == 1:
                step(qi, (n_inner - 1) * tq, slot, (qi, n_inner * tq, False))
                slot = 1 - slot
            step(qi, n_inner * tq, slot, after)
            slot = 1 - slot
        finish(qi)


def _fox(fqt, fk, fvt, kbias, crow4, tq):
    b, s, _ = fk.shape
    pairs = FOX_HEADS // 2
    nq = s // tq
    return pl.pallas_call(
        functools.partial(_fox_kernel, tq=tq),
        grid=(b, pairs),
        in_specs=[
            pl.BlockSpec((LANES, s), lambda bi, hp: (hp, bi)),
            pl.BlockSpec((1, s, LANES), lambda bi, hp: (bi, 0, hp)),
            pl.BlockSpec((LANES, s), lambda bi, hp: (hp, bi)),
            pl.BlockSpec((1, s, LANES), lambda bi, hp: (bi, 0, hp)),
            pl.BlockSpec((1, 1, 2, s), lambda bi, hp: (bi, hp, 0, 0)),
        ],
        out_specs=pl.BlockSpec((1, s, LANES), lambda bi, hp: (bi, 0, hp)),
        out_shape=jax.ShapeDtypeStruct((b, s, FOX_WIDTH), BF16),
        scratch_shapes=[pltpu.VMEM((2, s, LANES), BF16), pltpu.VMEM((2, FOX_VROWS, s), BF16),
                        pltpu.VMEM((nq, 2, LANES, tq), BF16),
                        pltpu.VMEM((nq, 2, 1, tq), F32), pltpu.VMEM((nq, 2, FOX_VROWS, tq), F32),
                        pltpu.VMEM((2, 2, tq, tq), F32), pltpu.VMEM((2, 2, 1, tq), F32)],
        compiler_params=pltpu.CompilerParams(
            dimension_semantics=("parallel", "parallel"),
            vmem_limit_bytes=VMEM_LIMIT),
        name="fox",
    )(fqt, fk, fvt, kbias, crow4)


def _tail_kernel(x_ref, g0_ref, b0_ref, ya_ref, yb_ref, sga_ref, sgb_ref, p_ref,
                 wa_ref, wb_ref, wo_ref, g1_ref, b1_ref,
                 w1_ref, w2_ref, wpg_ref, wp_ref, g2_ref, b2_ref, o_ref, *, alpha, fc):
    rows_per_group = x_ref.shape[0] // ROW_GROUPS
    groups = [slice(r * rows_per_group, (r + 1) * rows_per_group) for r in range(ROW_GROUPS)]

    def mixer(rows):
        h0 = _layer_norm(x_ref[rows, :], g0_ref[...], b0_ref[...])
        merged = (sga_ref[rows, :].astype(F32) * _dot(ya_ref[rows, :], wa_ref[...])
                  + sgb_ref[rows, :].astype(F32) * _dot(yb_ref[rows, :], wb_ref[...]))
        mix = _dot(merged.astype(BF16), wo_ref[...])
        return _layer_norm(alpha * h0 + mix, g1_ref[...], b1_ref[...])

    def channel_mixer(rows, h):
        hb = h.astype(BF16)
        sg = 0.5 + _half_tanh(_dot(hb, wpg_ref[...]))
        acc = alpha * h + sg * _dot(p_ref[rows, :].astype(BF16), wp_ref[...])
        for f in range(0, D_FF, fc):
            u = jnp.maximum(_dot(hb, w1_ref[:, f:f + fc]), 0.0)
            acc = acc + _dot((u * u).astype(BF16), w2_ref[f:f + fc, :])
        o_ref[rows, :] = _layer_norm(acc, g2_ref[...], b2_ref[...])

    h1 = [mixer(rows) for rows in groups]
    for rows, h in zip(groups, h1):
        channel_mixer(rows, h)


def _tail(x2, ln0_g, ln0_b, ya, yb, sga, sgb, p2, wa, wb, wo, ln1_g, ln1_b,
          w1, w2, wpg, wp, ln2_g, ln2_b, alpha, tm):
    t = x2.shape[0]
    tok = lambda w: pl.BlockSpec((tm, w), lambda i: (i, 0))
    consts = [wa, wb, wo, ln1_g, ln1_b, w1, w2, wpg, wp, ln2_g, ln2_b]
    return pl.pallas_call(
        functools.partial(_tail_kernel, alpha=alpha, fc=1024),
        grid=(t // tm,),
        in_specs=[tok(D_MODEL), _const_spec(ln0_g.shape), _const_spec(ln0_b.shape),
                  tok(HG_WIDTH), tok(FOX_WIDTH), tok(D_MODEL), tok(D_MODEL), tok(PLE_DIM)]
        + [_const_spec(a.shape) for a in consts],
        out_specs=tok(D_MODEL),
        out_shape=jax.ShapeDtypeStruct((t, D_MODEL), F32),
        compiler_params=pltpu.CompilerParams(
            dimension_semantics=("parallel",), vmem_limit_bytes=VMEM_LIMIT),
        name="tail",
    )(x2, ln0_g, ln0_b, ya, yb, sga, sgb, p2, wa, wb, wo, ln1_g, ln1_b,
      w1, w2, wpg, wp, ln2_g, ln2_b)


def kernel(x, p, ln0_g, ln0_b, w_in, hg_lb, hg_norm_g, fox_fb, w_a, w_b, w_o,
           ln1_g, ln1_b, w_ff1, w_ff2, w_pg, w_p, ln2_g, ln2_b):
    b, s, d = x.shape
    depth = w_in.shape[0]
    assert depth == 1 and d == D_MODEL and hg_lb.shape[0] == 2
    assert s % 512 == 0
    t = b * s
    alpha = (2.0 * depth) ** 0.25
    row = lambda a: a.reshape(1, -1).astype(F32)

    sizes = [HG_WIDTH] * 4 + [FOX_WIDTH] * 3 + [FOX_HEADS] + [D_MODEL] * 2
    offs = np.concatenate([[0], np.cumsum(sizes)])
    wi = w_in[0]
    parts = [wi[:, offs[i]:offs[i + 1]] for i in range(len(sizes))]
    w_hq, w_hf, w_hi, w_hg, w_fq, w_fk, w_fv, w_ff, w_ga, w_gb = parts
    w_ff = jnp.pad(w_ff, ((0, 0), (0, LANES - FOX_HEADS)))
    fb_pad = jnp.pad(fox_fb[0].astype(F32), (0, LANES - FOX_HEADS)).reshape(1, LANES)
    ws = [w.astype(BF16) for w in
          (w_hq, w_hf, w_hi, w_hg, w_fq.T, w_fk, w_fv.T, w_ff, w_ga, w_gb)]

    x2 = x.reshape(t, d)
    g0, b0 = row(ln0_g), row(ln0_b)
    (qa, g, kk, v, og, fqt, fk, fvt, lf, sga, sgb) = _inproj(
        x2, g0, b0, hg_lb.astype(F32), fb_pad, ws, tm=1024)

    kbias, crow = _foxcum(lf.reshape(b, s, LANES))
    crow4 = crow.reshape(b, FOX_HEADS // 2, 2, s)

    r3 = lambda a: a.reshape(b, s, a.shape[-1])
    ya = _hgrn(r3(qa), r3(g), r3(kk), r3(v), r3(og), hg_norm_g.astype(F32), tc=512)
    yb = _fox(fqt, r3(fk), fvt, kbias, crow4, tq=512)

    out = _tail(x2, g0, b0, ya.reshape(t, HG_WIDTH), yb.reshape(t, FOX_WIDTH), sga, sgb,
                p[0].reshape(t, PLE_DIM),
                w_a[0].astype(BF16), w_b[0].astype(BF16), w_o[0].astype(BF16),
                row(ln1_g[0]), row(ln1_b[0]),
                w_ff1[0].astype(BF16), w_ff2[0].astype(BF16),
                w_pg[0].astype(BF16), w_p[0].astype(BF16), row(ln2_g[0]), row(ln2_b[0]),
                alpha, tm=512)
    return out.reshape(b, s, d)
```

```python
import functools

import numpy as np
import jax
import jax.numpy as jnp
from jax import lax
from jax.experimental import pallas as pl
from jax.experimental.pallas import tpu as pltpu

F32 = jnp.float32
BF16 = jnp.bfloat16

D_MODEL = 1024
HG_WIDTH = 512
HG_KDIM = 128
HG_HEADS = HG_WIDTH // HG_KDIM
FOX_WIDTH = 512
FOX_HDIM = 64
FOX_HEADS = FOX_WIDTH // FOX_HDIM
D_FF = 4 * D_MODEL
PLE_DIM = 256
LN_EPS = 1e-5
RMS_EPS = 1e-6
LANES = 128
HG_CHUNK = 64
HG_LEVELS = 6
HG_HEADS_PER_STEP = 4
HG_FROM_START = HG_LEVELS - 1
HG_TO_END = HG_LEVELS
HG_DECAY_BLOCKS = HG_LEVELS + 1
NEG_BIG = -1e30
LOG2E = 1.4426950408889634
VMEM_LIMIT = 56 * 1024 * 1024
ROW_GROUPS = 2


def _layer_norm(x, g, b):
    mu = jnp.mean(x, axis=-1, keepdims=True)
    xc = x - mu
    var = jnp.mean(xc * xc, axis=-1, keepdims=True)
    return xc * lax.rsqrt(var + LN_EPS) * g + b


def _half_tanh(x):
    return 0.5 * jnp.tanh(0.5 * x)


def _split3(x):
    hi = x.astype(BF16)
    r = x - hi.astype(F32)
    mid = r.astype(BF16)
    lo = (r - mid.astype(F32)).astype(BF16)
    return hi, mid, lo


def _dot(a, b):
    return jnp.dot(a, b, preferred_element_type=F32)


def _dot_nt(a, b):
    return lax.dot_general(a, b, (((1,), (1,)), ((), ())), preferred_element_type=F32)


def _dot_tn(a, b):
    return lax.dot_general(a, b, (((0,), (0,)), ((), ())), preferred_element_type=F32)


def _inproj_kernel(x_ref, g0_ref, b0_ref, lb_ref, fb_ref,
                   w_hq, w_hf, w_hi, w_hg, w_fqt, w_fk, w_fvt, w_ff, w_ga, w_gb,
                   qa_ref, g_ref, kk_ref, v_ref, og_ref,
                   fqt_ref, fk_ref, fvt_ref, lf_ref, sga_ref, sgb_ref):
    lbl = lb_ref[...]
    e = jnp.exp(lbl - jnp.max(lbl, axis=0, keepdims=True))
    lb = e[0:1, :] / jnp.sum(e, axis=0, keepdims=True)

    tm = x_ref.shape[0]
    rows_per_group = tm // ROW_GROUPS
    for r in range(ROW_GROUPS):
        rows = slice(r * rows_per_group, (r + 1) * rows_per_group)
        h = _layer_norm(x_ref[rows, :], g0_ref[...], b0_ref[...]).astype(BF16)

        q = _dot(h, w_hq[...])
        qa_ref[rows, :] = (q * (0.5 + _half_tanh(q))).astype(BF16)

        tf = _half_tanh(_dot(h, w_hf[...]))
        g_ref[rows, :] = jnp.log(lb + (1.0 - lb) * (0.5 + tf))
        kk_ref[rows, :] = ((1.0 - lb) * (0.5 - tf)).astype(BF16)

        v_ref[rows, :] = _dot(h, w_hi[...]).astype(BF16)

        og = _dot(h, w_hg[...])
        og_ref[rows, :] = (og * (0.5 + _half_tanh(og))).astype(BF16)

        fqt_ref[:, rows] = (_dot_nt(w_fqt[...], h) * (FOX_HDIM ** -0.5 * LOG2E)).astype(BF16)
        fk_ref[rows, :] = _dot(h, w_fk[...]).astype(BF16)
        fvt_ref[:, rows] = _dot_nt(w_fvt[...], h).astype(BF16)

        z = _dot(h, w_ff[...]) + fb_ref[...]
        lf_ref[rows, :] = jnp.minimum(z, 0.0) - jnp.log(1.0 + jnp.exp(-jnp.abs(z)))

        sga_ref[rows, :] = (0.5 + _half_tanh(_dot(h, w_ga[...]))).astype(BF16)
        sgb_ref[rows, :] = (0.5 + _half_tanh(_dot(h, w_gb[...]))).astype(BF16)


def _const_spec(shape):
    nd = len(shape)
    return pl.BlockSpec(shape, lambda *_: (0,) * nd, pipeline_mode=pl.Buffered(1))


def _inproj(x2, ln0_g, ln0_b, hg_lb, fb_pad, ws, tm):
    t = x2.shape[0]
    tok = lambda w: pl.BlockSpec((tm, w), lambda i: (i, 0))
    out_shapes = [
        jax.ShapeDtypeStruct((t, HG_WIDTH), BF16),
        jax.ShapeDtypeStruct((t, HG_WIDTH), F32),
        jax.ShapeDtypeStruct((t, HG_WIDTH), BF16),
        jax.ShapeDtypeStruct((t, HG_WIDTH), BF16),
        jax.ShapeDtypeStruct((t, HG_WIDTH), BF16),
        jax.ShapeDtypeStruct((FOX_WIDTH, t), BF16),
        jax.ShapeDtypeStruct((t, FOX_WIDTH), BF16),
        jax.ShapeDtypeStruct((FOX_WIDTH, t), BF16),
        jax.ShapeDtypeStruct((t, LANES), F32),
        jax.ShapeDtypeStruct((t, D_MODEL), BF16),
        jax.ShapeDtypeStruct((t, D_MODEL), BF16),
    ]
    tok_t = pl.BlockSpec((FOX_WIDTH, tm), lambda i: (0, i))
    out_specs = [tok(HG_WIDTH)] * 5 + [
        tok_t, tok(FOX_WIDTH), tok_t, tok(LANES), tok(D_MODEL), tok(D_MODEL)]
    in_specs = [tok(D_MODEL), _const_spec(ln0_g.shape), _const_spec(ln0_b.shape),
                _const_spec(hg_lb.shape), _const_spec(fb_pad.shape)]
    in_specs += [_const_spec(w.shape) for w in ws]
    return pl.pallas_call(
        _inproj_kernel,
        grid=(t // tm,),
        in_specs=in_specs,
        out_specs=out_specs,
        out_shape=out_shapes,
        compiler_params=pltpu.CompilerParams(
            dimension_semantics=("parallel",), vmem_limit_bytes=VMEM_LIMIT),
        name="inproj",
    )(x2, ln0_g, ln0_b, hg_lb, fb_pad, *ws)


FOX_BIAS_BASE = (FOX_HDIM, 0)
BF16_SUBLANES = 16
FOX_VROWS = FOX_HDIM + BF16_SUBLANES


def _fox_bias_routing():
    pairs = FOX_HEADS // 2
    route = np.zeros((3 * LANES, pairs * LANES), np.float32)
    ones = np.zeros((1, pairs * LANES), np.float32)
    for head in range(FOX_HEADS):
        base = (head // 2) * LANES + FOX_BIAS_BASE[head % 2]
        for i in range(3):
            route[i * LANES + head, base + i] = 1.0
            ones[0, base + 3 + i] = 1.0
    return route, ones


def _foxcum_kernel(lf_ref, route_ref, ones_ref, kbias_ref, crow_ref, ccol_ref,
                   *, rows, route_rows):
    s = lf_ref.shape[1]
    ri = lax.broadcasted_iota(jnp.int32, (rows, rows), 0)
    ci = lax.broadcasted_iota(jnp.int32, (rows, rows), 1)
    tril = jnp.where(ci <= ri, 1.0, 0.0).astype(BF16)
    carry = jnp.zeros((1, LANES), F32)
    for i in range(s // rows):
        terms = jnp.concatenate(_split3(lf_ref[0, i * rows:(i + 1) * rows, :]), axis=1)
        sums = _dot(tril, terms)
        c = (sums[:, 0:LANES] + sums[:, LANES:2 * LANES]) + sums[:, 2 * LANES:3 * LANES] + carry
        ccol_ref[i * rows:(i + 1) * rows, :] = c * LOG2E
        carry = c[rows - 1:rows, :]
    crow_ref[0] = ccol_ref[...].T[0:FOX_HEADS, :]
    for i in range(s // route_rows):
        span = slice(i * route_rows, (i + 1) * route_rows)
        terms = jnp.concatenate(_split3(-ccol_ref[span, :]), axis=1)
        kbias_ref[0, span, :] = (_dot(terms, route_ref[...]) + ones_ref[...]).astype(BF16)


def _foxcum(lf3):
    b, s, _ = lf3.shape
    route, ones = _fox_bias_routing()
    route = jnp.asarray(route, BF16)
    ones = jnp.asarray(ones, F32)
    width = route.shape[1]
    return pl.pallas_call(
        functools.partial(_foxcum_kernel, rows=256, route_rows=min(s, 1024)),
        grid=(b,),
        in_specs=[pl.BlockSpec((1, s, LANES), lambda i: (i, 0, 0)),
                  _const_spec(route.shape), _const_spec(ones.shape)],
        out_specs=[pl.BlockSpec((1, s, width), lambda i: (i, 0, 0)),
                   pl.BlockSpec((1, FOX_HEADS, s), lambda i: (i, 0, 0))],
        out_shape=[jax.ShapeDtypeStruct((b, s, width), BF16),
                   jax.ShapeDtypeStruct((b, FOX_HEADS, s), F32)],
        scratch_shapes=[pltpu.VMEM((s, LANES), F32)],
        compiler_params=pltpu.CompilerParams(dimension_semantics=("parallel",)),
        name="foxcum",
    )(lf3, route, ones)


def _hgrn_decay_matrix():
    c = HG_CHUNK
    w = np.zeros((HG_DECAY_BLOCKS, c, c), np.float32)
    j = np.arange(c)
    for p in range(1, HG_LEVELS):
        n = 1 << p
        for t in range(c):
            start = t - t % n
            if t & n:
                w[p - 1, t] = (j >= start) & (j <= t)
            else:
                w[p - 1, t] = (j > t) & (j <= start + n - 1)
    for t in range(c):
        w[HG_FROM_START, t] = j <= t
        w[HG_TO_END, t] = j > t
    return np.tile(w.reshape(-1, c), (1, 3))


def _hgrn_kernel(w_ref, q_ref, g_ref, k_ref, v_ref, og_ref, ng_ref, y_ref, st_ref, d_ref,
                 *, nc):
    c = HG_CHUNK

    @pl.when(pl.program_id(2) == 0)
    def _():
        st_ref[...] = jnp.zeros_like(st_ref)

    def chunks_on_lanes(x):
        return jnp.concatenate([x[i * c:(i + 1) * c, :] for i in range(nc)], axis=1)

    def lane_chunk(x, i):
        return x[:, i * LANES:(i + 1) * LANES]

    ti = lax.broadcasted_iota(jnp.int32, (c, c), 0)
    si = lax.broadcasted_iota(jnp.int32, (c, c), 1)
    x = jnp.bitwise_xor(ti, si)
    level = jnp.zeros((c, c), jnp.int32)
    for p in range(HG_LEVELS):
        level = level + jnp.where(x >= (1 << p), 1, 0)
    level = jnp.where(ti >= si, level, -1)

    heads = range(HG_HEADS_PER_STEP)
    lanes = [slice(h * HG_KDIM, (h + 1) * HG_KDIM) for h in heads]

    def decay(h, block):
        return jnp.exp(d_ref[h, block * c:(block + 1) * c, :])

    for h in heads:
        g3 = jnp.concatenate(_split3(chunks_on_lanes(g_ref[0, :, lanes[h]])), axis=0)
        d_ref[h] = _dot(w_ref[...], g3)

    def within_chunks(h):
        q = chunks_on_lanes(q_ref[0, :, lanes[h]])
        k = chunks_on_lanes(k_ref[0, :, lanes[h]])
        q0 = q * jnp.exp(chunks_on_lanes(g_ref[0, :, lanes[h]])).astype(BF16)
        scores = []
        for i in range(nc):
            both = _dot_nt(jnp.concatenate([lane_chunk(q, i), lane_chunk(q0, i)], axis=0),
                           lane_chunk(k, i))
            scores.append(jnp.where(level == 0, both[0:c, :],
                                    jnp.where(level == 1, both[c:2 * c, :], 0.0)))
        for p in range(1, HG_LEVELS):
            e = decay(h, p - 1).astype(BF16)
            qt = q * e
            kt = k * e
            for i in range(nc):
                a = _dot_nt(lane_chunk(qt, i), lane_chunk(kt, i))
                scores[i] = jnp.where(level == p + 1, a, scores[i])
        return q, k, scores

    def state_chain(h, k):
        e_start = decay(h, HG_FROM_START)
        k_out = k * decay(h, HG_TO_END).astype(BF16)
        v = v_ref[0, :, lanes[h]]
        v_chunks = [v[i * c:(i + 1) * c, :] for i in range(nc)]
        updates = [_dot_tn(v_chunks[i], lane_chunk(k_out, i)) for i in range(nc)]
        st = st_ref[h]
        states = []
        for i in range(nc):
            states.append(st.astype(BF16))
            st = st * lane_chunk(e_start, i)[c - 1:c, :] + updates[i]
        st_ref[h] = st
        return e_start, v_chunks, states

    def outputs(h, q, scores, e_start, v_chunks, states):
        q_in = q * e_start.astype(BF16)
        ng = ng_ref[:, lanes[h]]
        for i in range(nc):
            o = (_dot(scores[i].astype(BF16), v_chunks[i])
                 + _dot_nt(lane_chunk(q_in, i), states[i]))
            o = o * lax.rsqrt(jnp.mean(o * o, axis=-1, keepdims=True) + RMS_EPS)
            gate = og_ref[0, i * c:(i + 1) * c, lanes[h]].astype(F32)
            y_ref[0, i * c:(i + 1) * c, lanes[h]] = (o * ng * gate).astype(BF16)

    within = {}
    chain = {}
    for h in heads:
        within[h] = within_chunks(h)
        chain[h] = state_chain(h, within[h][1])
    for h in heads:
        q, _, scores = within[h]
        outputs(h, q, scores, *chain[h])


def _hgrn(qa, g, kk, v, og, norm_g, tc):
    b, s, _ = qa.shape
    nc = tc // HG_CHUNK
    width = HG_HEADS_PER_STEP * HG_KDIM
    w = jnp.asarray(_hgrn_decay_matrix(), BF16)
    tok = pl.BlockSpec((1, tc, width), lambda bi, hi, si: (bi, si, hi))
    return pl.pallas_call(
        functools.partial(_hgrn_kernel, nc=nc),
        grid=(b, HG_HEADS // HG_HEADS_PER_STEP, s // tc),
        in_specs=[_const_spec(w.shape), tok, tok, tok, tok, tok,
                  pl.BlockSpec((1, width), lambda bi, hi, si: (0, hi))],
        out_specs=tok,
        out_shape=jax.ShapeDtypeStruct((b, s, HG_WIDTH), BF16),
        scratch_shapes=[pltpu.VMEM((HG_HEADS_PER_STEP, HG_KDIM, HG_KDIM), F32),
                        pltpu.VMEM((HG_HEADS_PER_STEP, HG_DECAY_BLOCKS * HG_CHUNK,
                                    nc * HG_KDIM), F32)],
        compiler_params=pltpu.CompilerParams(
            dimension_semantics=("parallel", "parallel", "arbitrary"),
            vmem_limit_bytes=VMEM_LIMIT),
        name="hgrn",
    )(w, qa, g, kk, v, og, norm_g)


def _fox_kernel(qt_ref, k_ref, vt_ref, kbias_ref, crow_ref, o_ref,
                kaug_ref, vaug_ref, qaug_ref, m_ref, acc_ref, s_ref, mx_ref, *, tq):
    s_len = k_ref.shape[1]
    nq = s_len // tq
    half = tq // 2
    aug_base = FOX_BIAS_BASE
    def build_keys_and_values():
        lane = lax.broadcasted_iota(jnp.int32, (s_len, LANES), 1)
        pad_row = lax.broadcasted_iota(jnp.int32, (FOX_VROWS - FOX_HDIM, s_len), 0)
        ones_rows = jnp.where(pad_row == 0, 1.0, 0.0).astype(BF16)
        k = k_ref[0]
        kbias = kbias_ref[0]
        for hh in range(2):
            is_own = (lane < FOX_HDIM) if hh == 0 else (lane >= FOX_HDIM)
            kaug_ref[hh] = jnp.where(is_own, k, kbias)
            vaug_ref[hh, 0:FOX_HDIM, :] = vt_ref[hh * FOX_HDIM:(hh + 1) * FOX_HDIM, :]
            vaug_ref[hh, FOX_HDIM:FOX_VROWS, :] = ones_rows

    row = lax.broadcasted_iota(jnp.int32, (LANES, tq), 0)

    def build_queries(qi):
        qt = qt_ref[:, qi * tq:(qi + 1) * tq].astype(F32)
        for hh in range(2):
            hi, mid, lo = _split3(crow_ref[0, 0, hh:hh + 1, qi * tq:qi * tq + LANES][:, 0:1])
            rel = row - aug_base[hh]
            aug = jnp.where(rel == 3, hi.astype(F32),
                            jnp.where(rel == 4, mid.astype(F32),
                                      jnp.where(rel == 5, lo.astype(F32),
                                                jnp.where(rel < 3, 1.0, 0.0))))
            aug = jnp.where(rel >= 0, aug, 0.0)
            is_own = (row < FOX_HDIM) if hh == 0 else (row >= FOX_HDIM)
            qaug_ref[qi, hh] = jnp.where(is_own, qt, aug).astype(BF16)

    build_keys_and_values()
    for qi in range(nq):
        build_queries(qi)
    m_ref[...] = jnp.full(m_ref.shape, NEG_BIG, F32)
    acc_ref[...] = jnp.zeros(acc_ref.shape, F32)

    def issue_scores(nxt, slot, part, mx):
        qi_next, k_next, diagonal = nxt
        rows = slice(part * half, (part + 1) * half)
        for hh in range(2):
            keys = kaug_ref[hh, pl.ds(k_next + part * half, half), :]
            if diagonal and part == 1:
                st = _dot(keys, qaug_ref[qi_next, hh, :, half:])
                ki = lax.broadcasted_iota(jnp.int32, (half, half), 0)
                ti = lax.broadcasted_iota(jnp.int32, (half, half), 1)
                st = jnp.concatenate([jnp.full((half, half), NEG_BIG, F32),
                                      jnp.where(ki <= ti, st, NEG_BIG)], axis=1)
            else:
                st = _dot(keys, qaug_ref[qi_next, hh])
                if diagonal:
                    ki = lax.broadcasted_iota(jnp.int32, (half, tq), 0)
                    ti = lax.broadcasted_iota(jnp.int32, (half, tq), 1)
                    st = jnp.where(ki <= ti, st, NEG_BIG)
            s_ref[slot, hh, rows, :] = st
            part_max = jnp.max(st, axis=0, keepdims=True)
            mx[hh] = part_max if part == 0 else jnp.maximum(mx[hh], part_max)

    def step(qi_cur, k_cur, cur, nxt):
        m_new, scales = [], []
        for hh in range(2):
            m_old = m_ref[qi_cur, hh]
            m_new.append(jnp.maximum(m_old, mx_ref[cur, hh]))
            scales.append(jnp.exp2(m_old - m_new[hh]))
            m_ref[qi_cur, hh] = m_new[hh]
        pv = [None, None]
        mx = [None, None]
        for part in range(2):
            rows = slice(part * half, (part + 1) * half)
            if nxt is not None:
                issue_scores(nxt, 1 - cur, part, mx)
            for hh in range(2):
                p = jnp.exp2(s_ref[cur, hh, rows, :] - m_new[hh]).astype(BF16)
                d = _dot(vaug_ref[hh, :, pl.ds(k_cur + part * half, half)], p)
                pv[hh] = d if part == 0 else pv[hh] + d
        for hh in range(2):
            acc_ref[qi_cur, hh] = scales[hh] * acc_ref[qi_cur, hh] + pv[hh]
            if nxt is not None:
                mx_ref[1 - cur, hh] = mx[hh]

    def finish(qi):
        heads = []
        for hh in range(2):
            acc = acc_ref[qi, hh]
            heads.append(acc[0:FOX_HDIM, :] / acc[FOX_HDIM:FOX_HDIM + 1, :])
        ot = jnp.concatenate(heads, axis=0)
        o_ref[0, qi * tq:(qi + 1) * tq, :] = ot.T.astype(BF16)

    def diagonal_of(qi):
        return (qi, qi * tq, True) if qi < nq else None

    mx0 = [None, None]
    for part in range(2):
        issue_scores(diagonal_of(0), 0, part, mx0)
    for hh in range(2):
        mx_ref[0, hh] = mx0[hh]

    slot = 0
    for qi in range(nq):
        after = diagonal_of(qi + 1)
        step(qi, qi * tq, slot, (qi, 0, False) if qi > 0 else after)
        slot = 1 - slot
        if qi > 0:
            n_inner = qi - 1
            if n_inner // 2 > 0:
                first = slot

                def two_steps(jj, carry, qi=qi, first=first):
                    k0 = pl.multiple_of(2 * jj * tq, tq)
                    step(qi, k0, first, (qi, k0 + tq, False))
                    step(qi, k0 + tq, 1 - first, (qi, k0 + 2 * tq, False))
                    return carry

                lax.fori_loop(0, n_inner // 2, two_steps, 0)
            if n_inner % 2 == 1:
                step(qi, (n_inner - 1) * tq, slot, (qi, n_inner * tq, False))
                slot = 1 - slot
            step(qi, n_inner * tq, slot, after)
            slot = 1 - slot
        finish(qi)


def _fox(fqt, fk, fvt, kbias, crow4, tq):
    b, s, _ = fk.shape
    pairs = FOX_HEADS // 2
    nq = s // tq
    return pl.pallas_call(
        functools.partial(_fox_kernel, tq=tq),
        grid=(b, pairs),
        in_specs=[
            pl.BlockSpec((LANES, s), lambda bi, hp: (hp, bi)),
            pl.BlockSpec((1, s, LANES), lambda bi, hp: (bi, 0, hp)),
            pl.BlockSpec((LANES, s), lambda bi, hp: (hp, bi)),
            pl.BlockSpec((1, s, LANES), lambda bi, hp: (bi, 0, hp)),
            pl.BlockSpec((1, 1, 2, s), lambda bi, hp: (bi, hp, 0, 0)),
        ],
        out_specs=pl.BlockSpec((1, s, LANES), lambda bi, hp: (bi, 0, hp)),
        out_shape=jax.ShapeDtypeStruct((b, s, FOX_WIDTH), BF16),
        scratch_shapes=[pltpu.VMEM((2, s, LANES), BF16), pltpu.VMEM((2, FOX_VROWS, s), BF16),
                        pltpu.VMEM((nq, 2, LANES, tq), BF16),
                        pltpu.VMEM((nq, 2, 1, tq), F32), pltpu.VMEM((nq, 2, FOX_VROWS, tq), F32),
                        pltpu.VMEM((2, 2, tq, tq), F32), pltpu.VMEM((2, 2, 1, tq), F32)],
        compiler_params=pltpu.CompilerParams(
            dimension_semantics=("parallel", "parallel"),
            vmem_limit_bytes=VMEM_LIMIT),
        name="fox",
    )(fqt, fk, fvt, kbias, crow4)


def _tail_kernel(x_ref, g0_ref, b0_ref, ya_ref, yb_ref, sga_ref, sgb_ref, p_ref,
                 wa_ref, wb_ref, wo_ref, g1_ref, b1_ref,
                 w1_ref, w2_ref, wpg_ref, wp_ref, g2_ref, b2_ref, o_ref, *, alpha, fc):
    rows_per_group = x_ref.shape[0] // ROW_GROUPS
    groups = [slice(r * rows_per_group, (r + 1) * rows_per_group) for r in range(ROW_GROUPS)]

    def mixer(rows):
        h0 = _layer_norm(x_ref[rows, :], g0_ref[...], b0_ref[...])
        merged = (sga_ref[rows, :].astype(F32) * _dot(ya_ref[rows, :], wa_ref[...])
                  + sgb_ref[rows, :].astype(F32) * _dot(yb_ref[rows, :], wb_ref[...]))
        mix = _dot(merged.astype(BF16), wo_ref[...])
        return _layer_norm(alpha * h0 + mix, g1_ref[...], b1_ref[...])

    def channel_mixer(rows, h):
        hb = h.astype(BF16)
        sg = 0.5 + _half_tanh(_dot(hb, wpg_ref[...]))
        acc = alpha * h + sg * _dot(p_ref[rows, :].astype(BF16), wp_ref[...])
        for f in range(0, D_FF, fc):
            u = jnp.maximum(_dot(hb, w1_ref[:, f:f + fc]), 0.0)
            acc = acc + _dot((u * u).astype(BF16), w2_ref[f:f + fc, :])
        o_ref[rows, :] = _layer_norm(acc, g2_ref[...], b2_ref[...])

    h1 = [mixer(rows) for rows in groups]
    for rows, h in zip(groups, h1):
        channel_mixer(rows, h)


def _tail(x2, ln0_g, ln0_b, ya, yb, sga, sgb, p2, wa, wb, wo, ln1_g, ln1_b,
          w1, w2, wpg, wp, ln2_g, ln2_b, alpha, tm):
    t = x2.shape[0]
    tok = lambda w: pl.BlockSpec((tm, w), lambda i: (i, 0))
    consts = [wa, wb, wo, ln1_g, ln1_b, w1, w2, wpg, wp, ln2_g, ln2_b]
    return pl.pallas_call(
        functools.partial(_tail_kernel, alpha=alpha, fc=1024),
        grid=(t // tm,),
        in_specs=[tok(D_MODEL), _const_spec(ln0_g.shape), _const_spec(ln0_b.shape),
                  tok(HG_WIDTH), tok(FOX_WIDTH), tok(D_MODEL), tok(D_MODEL), tok(PLE_DIM)]
        + [_const_spec(a.shape) for a in consts],
        out_specs=tok(D_MODEL),
        out_shape=jax.ShapeDtypeStruct((t, D_MODEL), F32),
        compiler_params=pltpu.CompilerParams(
            dimension_semantics=("parallel",), vmem_limit_bytes=VMEM_LIMIT),
        name="tail",
    )(x2, ln0_g, ln0_b, ya, yb, sga, sgb, p2, wa, wb, wo, ln1_g, ln1_b,
      w1, w2, wpg, wp, ln2_g, ln2_b)


def kernel(x, p, ln0_g, ln0_b, w_in, hg_lb, hg_norm_g, fox_fb, w_a, w_b, w_o,
           ln1_g, ln1_b, w_ff1, w_ff2, w_pg, w_p, ln2_g, ln2_b):
    b, s, d = x.shape
    depth = w_in.shape[0]
    assert depth == 1 and d == D_MODEL and hg_lb.shape[0] == 2
    assert s % 512 == 0
    t = b * s
    alpha = (2.0 * depth) ** 0.25
    row = lambda a: a.reshape(1, -1).astype(F32)

    sizes = [HG_WIDTH] * 4 + [FOX_WIDTH] * 3 + [FOX_HEADS] + [D_MODEL] * 2
    offs = np.concatenate([[0], np.cumsum(sizes)])
    wi = w_in[0]
    parts = [wi[:, offs[i]:offs[i + 1]] for i in range(len(sizes))]
    w_hq, w_hf, w_hi, w_hg, w_fq, w_fk, w_fv, w_ff, w_ga, w_gb = parts
    w_ff = jnp.pad(w_ff, ((0, 0), (0, LANES - FOX_HEADS)))
    fb_pad = jnp.pad(fox_fb[0].astype(F32), (0, LANES - FOX_HEADS)).reshape(1, LANES)
    ws = [w.astype(BF16) for w in
          (w_hq, w_hf, w_hi, w_hg, w_fq.T, w_fk, w_fv.T, w_ff, w_ga, w_gb)]

    x2 = x.reshape(t, d)
    g0, b0 = row(ln0_g), row(ln0_b)
    (qa, g, kk, v, og, fqt, fk, fvt, lf, sga, sgb) = _inproj(
        x2, g0, b0, hg_lb.astype(F32), fb_pad, ws, tm=1024)

    kbias, crow = _foxcum(lf.reshape(b, s, LANES))
    crow4 = crow.reshape(b, FOX_HEADS // 2, 2, s)

    r3 = lambda a: a.reshape(b, s, a.shape[-1])
    ya = _hgrn(r3(qa), r3(g), r3(kk), r3(v), r3(og), hg_norm_g.astype(F32), tc=1024)
    yb = _fox(fqt, r3(fk), fvt, kbias, crow4, tq=512)

    out = _tail(x2, g0, b0, ya.reshape(t, HG_WIDTH), yb.reshape(t, FOX_WIDTH), sga, sgb,
                p[0].reshape(t, PLE_DIM),
                w_a[0].astype(BF16), w_b[0].astype(BF16), w_o[0].astype(BF16),
                row(ln1_g[0]), row(ln1_b[0]),
                w_ff1[0].astype(BF16), w_ff2[0].astype(BF16),
                w_pg[0].astype(BF16), w_p[0].astype(BF16), row(ln2_g[0]), row(ln2_b[0]),
                alpha, tm=512)
    return out.reshape(b, s, d)
```
